```python
import jax, jax.numpy as jnp
from jax import lax
import numpy as np

D_MODEL = 4096
BATCH = 2
SEQ = 8192
DEPTH = 2

CHUNK = 64
Q_BLOCK = 128
N_A_LAYERS = DEPTH // 2
N_B_LAYERS = DEPTH - N_A_LAYERS
HEAD_DIM = 128
FOX_HEADS = D_MODEL // HEAD_DIM
FOX_HEAD_DIM = HEAD_DIM
MLA_HEADS = D_MODEL // HEAD_DIM
MLA_NOPE_DIM = 128
MLA_ROPE_DIM = 64
MLA_V_DIM = 128
MLA_Q_LORA = 1024
MLA_KV_LORA = 512
ROPE_THETA = 10000.0
D_FF = 11008
CONV_WIDTH = 3
DEEPNORM_ALPHA = (2.0 * DEPTH) ** 0.25
DEEPNORM_BETA = (8.0 * DEPTH) ** -0.25
LN_EPS = 1e-5
RMS_EPS = 1e-6
FORGET_BIAS_MEAN = 3.0
ADA_INIT_GAIN = 0.1

kernel_name = "fox_mla_yoco_convffn_deepnorm_adaln"


def _layer_norm(x, g, b):
    xf = x.astype(jnp.float32)
    mu = jnp.mean(xf, axis=-1, keepdims=True)
    var = jnp.mean(jnp.square(xf - mu), axis=-1, keepdims=True)
    return ((xf - mu) * lax.rsqrt(var + LN_EPS) * g + b).astype(x.dtype)


def _rms_norm(x, g):
    xf = x.astype(jnp.float32)
    return (xf * lax.rsqrt(jnp.mean(xf * xf, axis=-1, keepdims=True) + RMS_EPS) * g).astype(x.dtype)


def _rope_tables(seq):
    inv = ROPE_THETA ** (-jnp.arange(0, MLA_ROPE_DIM, 2, dtype=jnp.float32) / MLA_ROPE_DIM)
    ang = jnp.arange(seq, dtype=jnp.float32)[:, None] * inv[None, :]
    return jnp.cos(ang), jnp.sin(ang)


def _apply_rope(x, cos, sin):
    x1, x2 = jnp.split(x.astype(jnp.float32), 2, axis=-1)
    shape = (1, cos.shape[0]) + (1,) * (x.ndim - 3) + (cos.shape[1],)
    c, s = cos.reshape(shape), sin.reshape(shape)
    return jnp.concatenate([x1 * c - x2 * s, x1 * s + x2 * c], axis=-1).astype(x.dtype)


def _sweep_query_blocks(score_block, v):
    b, s, h, dv = v.shape

    def one_block(i):
        p = jax.nn.softmax(score_block(i * Q_BLOCK), axis=-1).astype(v.dtype)
        return jnp.einsum("bhqk,bkhd->bqhd", p, v)

    out = lax.map(one_block, jnp.arange(s // Q_BLOCK))
    return jnp.moveaxis(out, 0, 1).reshape(b, s, h * dv)


def _forgetting_attention(h, w_qkv, w_f, b_f, w_o):
    b, s, _ = h.shape
    qkv = (h @ w_qkv).reshape(b, s, 3, FOX_HEADS, FOX_HEAD_DIM)
    q, k, v = qkv[:, :, 0], qkv[:, :, 1], qkv[:, :, 2]
    log_f = jax.nn.log_sigmoid((h @ w_f + b_f).astype(jnp.float32))
    cum = jnp.cumsum(log_f, axis=1).transpose(0, 2, 1)
    scale = FOX_HEAD_DIM ** -0.5
    key_pos = jnp.arange(s)

    def score_block(start):
        qb = lax.dynamic_slice_in_dim(q, start, Q_BLOCK, axis=1)
        cq = lax.dynamic_slice_in_dim(cum, start, Q_BLOCK, axis=2)
        logits = jnp.einsum("bqhd,bkhd->bhqk", qb, k, preferred_element_type=jnp.float32) * scale
        logits = logits + (cq[..., :, None] - cum[..., None, :])
        mask = key_pos[None, :] <= (start + jnp.arange(Q_BLOCK))[:, None]
        return jnp.where(mask, logits, -jnp.inf)

    return _sweep_query_blocks(score_block, v) @ w_o


def _mla_shared_kv(hs, w_dkv, kv_norm, w_ukv, cos, sin):
    b, s, _ = hs.shape
    ckv_kr = hs @ w_dkv
    c_kv = _rms_norm(ckv_kr[..., :MLA_KV_LORA], kv_norm)
    k_rope = _apply_rope(ckv_kr[..., MLA_KV_LORA:], cos, sin)
    kv = (c_kv @ w_ukv).reshape(b, s, MLA_HEADS, MLA_NOPE_DIM + MLA_V_DIM)
    return kv[..., :MLA_NOPE_DIM], k_rope, kv[..., MLA_NOPE_DIM:]


def _mla_attention(h, k_nope, k_rope, v, w_dq, q_norm, w_uq, w_o, cos, sin):
    b, s, _ = h.shape
    q = (_rms_norm(h @ w_dq, q_norm) @ w_uq).reshape(b, s, MLA_HEADS, MLA_NOPE_DIM + MLA_ROPE_DIM)
    q_nope = q[..., :MLA_NOPE_DIM]
    q_rope = _apply_rope(q[..., MLA_NOPE_DIM:], cos, sin)
    scale = (MLA_NOPE_DIM + MLA_ROPE_DIM) ** -0.5
    key_chunk = jnp.arange(s) // CHUNK

    def score_block(start):
        qn = lax.dynamic_slice_in_dim(q_nope, start, Q_BLOCK, axis=1)
        qr = lax.dynamic_slice_in_dim(q_rope, start, Q_BLOCK, axis=1)
        logits = (jnp.einsum("bqhd,bkhd->bhqk", qn, k_nope, preferred_element_type=jnp.float32)
                  + jnp.einsum("bqhr,bkr->bhqk", qr, k_rope, preferred_element_type=jnp.float32)) * scale
        q_chunk = (start + jnp.arange(Q_BLOCK)) // CHUNK
        mask = key_chunk[None, :] <= q_chunk[:, None]
        return jnp.where(mask, logits, -jnp.inf)

    return _sweep_query_blocks(score_block, v) @ w_o


def _conv_ffn(h, w_up, conv_w, conv_b, w_down):
    s = h.shape[1]
    u = h @ w_up
    u_pad = jnp.pad(u, ((0, 0), (CONV_WIDTH - 1, 0), (0, 0)))
    u = sum((conv_w[j] * u_pad[:, j:j + s] for j in range(CONV_WIDTH)), conv_b)
    a, g = jnp.split(u, 2, axis=-1)
    return (jax.nn.silu(g) * a) @ w_down


def _modulate(ada, x):
    shift, scale, gate = jnp.split(ada, 3, axis=-1)
    return x * (1.0 + scale[:, None]) + shift[:, None], 1.0 + gate[:, None]


def setup_inputs(seed: int = 0) -> dict:
    key = jax.random.key(seed)
    ks = jax.random.split(key, 22)
    D = D_MODEL

    def nrm(k, shape, fan_in, gain=1.0):
        return jax.random.normal(k, shape, jnp.float32) * (gain * fan_in ** -0.5)

    def small(k, shape, s=0.02):
        return s * jax.random.normal(k, shape, jnp.float32)

    return {
        "x": jax.random.normal(ks[0], (BATCH, SEQ, D), jnp.float32),
        "c": jax.random.normal(ks[1], (BATCH, D), jnp.float32),
        "ada_w": nrm(ks[2], (DEPTH, 2, D, 3 * D), D, ADA_INIT_GAIN),
        "ada_b": small(ks[3], (DEPTH, 2, 3 * D)),
        "ln_g": 1.0 + small(ks[4], (DEPTH, 2, D)),
        "ln_b": small(ks[5], (DEPTH, 2, D)),
        "fox_w_qkv": nrm(ks[6], (N_A_LAYERS, D, 3 * FOX_HEADS * FOX_HEAD_DIM), D),
        "fox_w_f": nrm(ks[7], (N_A_LAYERS, D, FOX_HEADS), D),
        "fox_b_f": FORGET_BIAS_MEAN + small(ks[8], (N_A_LAYERS, FOX_HEADS), 0.5),
        "fox_w_o": nrm(ks[9], (N_A_LAYERS, FOX_HEADS * FOX_HEAD_DIM, D), FOX_HEADS * FOX_HEAD_DIM, DEEPNORM_BETA),
        "mla_w_dq": nrm(ks[10], (N_B_LAYERS, D, MLA_Q_LORA), D),
        "mla_q_norm": 1.0 + small(ks[11], (N_B_LAYERS, MLA_Q_LORA)),
        "mla_w_uq": nrm(ks[12], (N_B_LAYERS, MLA_Q_LORA, MLA_HEADS * (MLA_NOPE_DIM + MLA_ROPE_DIM)), MLA_Q_LORA),
        "mla_w_o": nrm(ks[13], (N_B_LAYERS, MLA_HEADS * MLA_V_DIM, D), MLA_HEADS * MLA_V_DIM, DEEPNORM_BETA),
        "mla_w_dkv": nrm(ks[14], (D, MLA_KV_LORA + MLA_ROPE_DIM), D),
        "mla_kv_norm": 1.0 + small(ks[15], (MLA_KV_LORA,)),
        "mla_w_ukv": nrm(ks[16], (MLA_KV_LORA, MLA_HEADS * (MLA_NOPE_DIM + MLA_V_DIM)), MLA_KV_LORA),
        "ffn_w_up": nrm(ks[17], (DEPTH, D, 2 * D_FF), D),
        "ffn_conv_w": nrm(ks[18], (DEPTH, CONV_WIDTH, 2 * D_FF), CONV_WIDTH),
        "ffn_conv_b": small(ks[19], (DEPTH, 2 * D_FF)),
        "ffn_w_down": nrm(ks[20], (DEPTH, D_FF, D), D_FF, DEEPNORM_BETA),
    }


def reference(x, c, ada_w, ada_b, ln_g, ln_b, fox_w_qkv, fox_w_f, fox_b_f, fox_w_o,
              mla_w_dq, mla_q_norm, mla_w_uq, mla_w_o, mla_w_dkv, mla_kv_norm, mla_w_ukv,
              ffn_w_up, ffn_conv_w, ffn_conv_b, ffn_w_down):
    cos, sin = _rope_tables(x.shape[1])
    c_act = jax.nn.silu(c)
    k_nope = k_rope = v_shared = None
    for layer in range(DEPTH):
        ada = jnp.einsum("bd,mde->mbe", c_act, ada_w[layer]) + ada_b[layer][:, None]
        h, gate = _modulate(ada[0], x)
        if layer < N_A_LAYERS:
            a = layer
            mix = _forgetting_attention(h, fox_w_qkv[a], fox_w_f[a], fox_b_f[a], fox_w_o[a])
        else:
            if layer == N_A_LAYERS:
                k_nope, k_rope, v_shared = _mla_shared_kv(x, mla_w_dkv, mla_kv_norm, mla_w_ukv, cos, sin)
            j = layer - N_A_LAYERS
            mix = _mla_attention(h, k_nope, k_rope, v_shared, mla_w_dq[j], mla_q_norm[j],
                                 mla_w_uq[j], mla_w_o[j], cos, sin)
        x = _layer_norm(DEEPNORM_ALPHA * x + gate * mix, ln_g[layer, 0], ln_b[layer, 0])
        h, gate = _modulate(ada[1], x)
        ffn = _conv_ffn(h, ffn_w_up[layer], ffn_conv_w[layer], ffn_conv_b[layer], ffn_w_down[layer])
        x = _layer_norm(DEEPNORM_ALPHA * x + gate * ffn, ln_g[layer, 1], ln_b[layer, 1])
    return x
```

```python
import functools

import jax
import jax.numpy as jnp
from jax import lax
from jax.experimental import pallas as pl
from jax.experimental.pallas import tpu as pltpu

F32 = jnp.float32
BF16 = jnp.bfloat16

HEAD_DIM = 128
MLA_ROPE_DIM = 64
MLA_QK_PAD = 256
CHUNK = 64
CONV_WIDTH = 3
ROPE_THETA = 10000.0
LN_EPS = 1e-5
RMS_EPS = 1e-6
LANES = 128
SUBLANES = 8
VMEM_LIMIT_BYTES = 56 * 1024 * 1024


def _params(n_grid_dims):
    return pltpu.CompilerParams(
        dimension_semantics=("arbitrary",) * n_grid_dims,
        vmem_limit_bytes=VMEM_LIMIT_BYTES)


def _tile(n, want):
    t = min(n, want)
    while n % t:
        t //= 2
    return t


def _ada_kernel(c_ref, w_ref, b_ref, o_ref):
    c = c_ref[...]
    c_act = (c * jax.nn.sigmoid(c)).astype(BF16)
    o_ref[0] = jnp.dot(c_act, w_ref[0, 0].astype(BF16), preferred_element_type=F32) + b_ref[0]


def _ada_all(c, ada_w, ada_b):
    depth, two, d, d3 = ada_w.shape
    b = c.shape[0]
    c_pad = jnp.pad(c, ((0, SUBLANES - b), (0, 0)))
    tn = _tile(d3, 512)
    out = pl.pallas_call(
        _ada_kernel,
        out_shape=jax.ShapeDtypeStruct((depth * two, SUBLANES, d3), F32),
        grid=(depth * two, d3 // tn),
        in_specs=[
            pl.BlockSpec((SUBLANES, d), lambda lm, j: (0, 0)),
            pl.BlockSpec((1, 1, d, tn), lambda lm, j: (lm // 2, lm % 2, 0, j)),
            pl.BlockSpec((1, 1, tn), lambda lm, j: (lm, 0, j)),
        ],
        out_specs=pl.BlockSpec((1, SUBLANES, tn), lambda lm, j: (lm, 0, j)),
        compiler_params=_params(2),
        name="ada_ln",
    )(c_pad, ada_w, ada_b.reshape(depth * two, 1, d3))
    return out[:, :b]


def _split_ada(ada_lm, d):
    return (ada_lm[:, None, 0:d], ada_lm[:, None, d:2 * d], ada_lm[:, None, 2 * d:3 * d])


def _modulate_kernel(x_ref, scale_ref, shift_ref, h_ref):
    h_ref[...] = (x_ref[...] * (1.0 + scale_ref[0]) + shift_ref[0]).astype(h_ref.dtype)


def _modulate(x2d, scale, shift, batch):
    m, d = x2d.shape
    s = m // batch
    ts = _tile(s, 512)
    ns = s // ts
    return pl.pallas_call(
        _modulate_kernel,
        out_shape=jax.ShapeDtypeStruct((m, d), BF16),
        grid=(batch, ns),
        in_specs=[
            pl.BlockSpec((ts, d), lambda b, i: (b * ns + i, 0)),
            pl.BlockSpec((1, 1, d), lambda b, i: (b, 0, 0)),
            pl.BlockSpec((1, 1, d), lambda b, i: (b, 0, 0)),
        ],
        out_specs=pl.BlockSpec((ts, d), lambda b, i: (b * ns + i, 0)),
        compiler_params=_params(2),
        name="modulate",
    )(x2d, scale, shift)


def _ln_kernel(*refs, alpha, with_h, with_xb):
    x_ref, mix_ref, gate_ref, g_ref, b_ref = refs[:5]
    refs = refs[5:]
    if with_h:
        scale_ref, shift_ref = refs[:2]
        refs = refs[2:]
    xo_ref = refs[0]
    refs = refs[1:]
    y = alpha * x_ref[...] + (1.0 + gate_ref[0]) * mix_ref[...]
    mu = jnp.mean(y, axis=-1, keepdims=True)
    dlt = y - mu
    var = jnp.mean(dlt * dlt, axis=-1, keepdims=True)
    xn = dlt * lax.rsqrt(var + LN_EPS) * g_ref[...] + b_ref[...]
    xo_ref[...] = xn
    if with_h:
        refs[0][...] = (xn * (1.0 + scale_ref[0]) + shift_ref[0]).astype(BF16)
        refs = refs[1:]
    if with_xb:
        refs[0][...] = xn.astype(BF16)


def _residual_ln(x2d, mix, gate, ln_g, ln_b, batch, alpha, next_mod=None, with_xb=False):
    m, d = x2d.shape
    s = m // batch
    ts = _tile(s, 256)
    ns = s // ts
    row = pl.BlockSpec((ts, d), lambda b, i: (b * ns + i, 0))
    per_batch = pl.BlockSpec((1, 1, d), lambda b, i: (b, 0, 0))
    vec = pl.BlockSpec((1, d), lambda b, i: (0, 0))
    with_h = next_mod is not None
    args = [x2d, mix, gate, ln_g.reshape(1, d), ln_b.reshape(1, d)]
    in_specs = [row, row, per_batch, vec, vec]
    out_shape = [jax.ShapeDtypeStruct((m, d), F32)]
    out_specs = [row]
    if with_h:
        args += [next_mod[0], next_mod[1]]
        in_specs += [per_batch, per_batch]
        out_shape.append(jax.ShapeDtypeStruct((m, d), BF16))
        out_specs.append(row)
    if with_xb:
        out_shape.append(jax.ShapeDtypeStruct((m, d), BF16))
        out_specs.append(row)
    return pl.pallas_call(
        functools.partial(_ln_kernel, alpha=alpha, with_h=with_h, with_xb=with_xb),
        out_shape=out_shape,
        grid=(batch, ns),
        in_specs=in_specs,
        out_specs=out_specs,
        compiler_params=_params(2),
        name="residual_ln",
    )(*args)


def _mm_kernel(a_ref, w_ref, o_ref, *, scale, n_scaled_tiles):
    acc = jnp.dot(a_ref[...], w_ref[...], preferred_element_type=F32)
    if n_scaled_tiles:
        acc = acc * jnp.where(pl.program_id(1) < n_scaled_tiles, scale, 1.0)
    o_ref[...] = acc.astype(o_ref.dtype)


def _mm(a, w, out_dtype, tm, tn, scale=1.0, n_scaled_cols=0, name="mm"):
    m, k = a.shape
    n = w.shape[1]
    tm, tn = _tile(m, tm), _tile(n, tn)
    if n_scaled_cols:
        tn = _tile(n_scaled_cols, tn)
        assert n % tn == 0
    return pl.pallas_call(
        functools.partial(_mm_kernel, scale=scale, n_scaled_tiles=n_scaled_cols // tn),
        out_shape=jax.ShapeDtypeStruct((m, n), out_dtype),
        grid=(m // tm, n // tn),
        in_specs=[
            pl.BlockSpec((tm, k), lambda i, j: (i, 0)),
            pl.BlockSpec((k, tn), lambda i, j: (0, j)),
        ],
        out_specs=pl.BlockSpec((tm, tn), lambda i, j: (i, j)),
        compiler_params=_params(2),
        name=name,
    )(a, w)


def _fgate_kernel(h_ref, wf_ref, bf_ref, cum_ref, cum_t_ref, carry_ref, *, tm):
    @pl.when(pl.program_id(1) == 0)
    def _():
        carry_ref[...] = jnp.zeros_like(carry_ref)

    f = jnp.dot(h_ref[...], wf_ref[...], preferred_element_type=F32) + bf_ref[...]
    log_f = jnp.minimum(f, 0.0) - jnp.log1p(jnp.exp(-jnp.abs(f)))
    hi = log_f.astype(BF16)
    r1 = log_f - hi.astype(F32)
    mid = r1.astype(BF16)
    lo = (r1 - mid.astype(F32)).astype(BF16)
    row = lax.broadcasted_iota(jnp.int32, (tm, tm), 0)
    col = lax.broadcasted_iota(jnp.int32, (tm, tm), 1)
    tril = jnp.where(col <= row, 1.0, 0.0).astype(BF16)
    cum = (jnp.dot(tril, hi, preferred_element_type=F32)
           + jnp.dot(tril, mid, preferred_element_type=F32)
           + jnp.dot(tril, lo, preferred_element_type=F32)) + carry_ref[...]
    carry_ref[...] = cum[tm - 1:tm, :]
    cum_ref[0] = cum
    cum_t_ref[0] = cum.T


def _forget_cumsum(h, w_f, b_f, batch):
    m, d = h.shape
    s = m // batch
    heads = w_f.shape[1]
    tm = _tile(s, 512)
    ns = s // tm
    wf = jnp.pad(w_f, ((0, 0), (0, LANES - heads))).astype(BF16)
    bf = jnp.pad(b_f, (0, LANES - heads)).reshape(1, LANES)
    return pl.pallas_call(
        functools.partial(_fgate_kernel, tm=tm),
        out_shape=[jax.ShapeDtypeStruct((batch, s, LANES), F32),
                   jax.ShapeDtypeStruct((batch, LANES, s), F32)],
        grid=(batch, ns),
        in_specs=[
            pl.BlockSpec((tm, d), lambda b, i: (b * ns + i, 0)),
            pl.BlockSpec((d, LANES), lambda b, i: (0, 0)),
            pl.BlockSpec((1, LANES), lambda b, i: (0, 0)),
        ],
        out_specs=[pl.BlockSpec((1, tm, LANES), lambda b, i: (b, i, 0)),
                   pl.BlockSpec((1, LANES, tm), lambda b, i: (b, 0, i))],
        scratch_shapes=[pltpu.VMEM((1, LANES), F32)],
        compiler_params=_params(2),
        name="forget_cumsum",
    )(h, wf, bf)


def _online_softmax_step(s, v, carry):
    m, l, acc = carry
    m_new = jnp.maximum(m, jnp.max(s, axis=1, keepdims=True))
    alpha = jnp.exp(m - m_new)
    p = jnp.exp(s - m_new)
    l = alpha * l + jnp.sum(p, axis=1, keepdims=True)
    acc = alpha * acc + jnp.dot(p.astype(BF16), v, preferred_element_type=F32)
    return m_new, l, acc


def _softmax_init(tq):
    return (jnp.full((tq, 1), -jnp.inf, F32), jnp.zeros((tq, 1), F32), jnp.zeros((tq, HEAD_DIM), F32))


def _nt_dot(q, k):
    return lax.dot_general(q, k, (((1,), (1,)), ((), ())), preferred_element_type=F32)


def _fox_attn_kernel(q_ref, k_ref, v_ref, cumc_ref, cumr_ref, o_ref, *, t):
    head = pl.program_id(1)
    qi = pl.program_id(2)
    q = q_ref[...]
    lane = lax.broadcasted_iota(jnp.int32, (t, LANES), 1)
    cq = jnp.sum(jnp.where(lane == head, cumc_ref[0], 0.0), axis=1, keepdims=True)

    def logits(j):
        start = pl.multiple_of(j * t, t)
        s = _nt_dot(q, k_ref[pl.ds(start, t), :])
        return s + (cq - cumr_ref[0, pl.ds(j, 1), :]), v_ref[pl.ds(start, t), :]

    def full_tile(j, carry):
        s, v = logits(j)
        return _online_softmax_step(s, v, carry)

    carry = lax.fori_loop(0, qi, full_tile, _softmax_init(t))
    s, v = logits(qi)
    row = lax.broadcasted_iota(jnp.int32, (t, t), 0)
    col = lax.broadcasted_iota(jnp.int32, (t, t), 1)
    s = jnp.where(col <= row, s, -jnp.inf)
    _, l, acc = _online_softmax_step(s, v, carry)
    o_ref[...] = (acc / l).astype(o_ref.dtype)


def _fox_attention(qkv, cum, cum_t, batch, heads):
    m = qkv.shape[0]
    s = m // batch
    t = _tile(s, 512)
    nq = s // t
    cum_rows = cum_t.reshape(batch * LANES, nq, t)
    return pl.pallas_call(
        functools.partial(_fox_attn_kernel, t=t),
        out_shape=jax.ShapeDtypeStruct((m, heads * HEAD_DIM), BF16),
        grid=(batch, heads, nq),
        in_specs=[
            pl.BlockSpec((t, HEAD_DIM), lambda b, h, i: (b * nq + i, h)),
            pl.BlockSpec((s, HEAD_DIM), lambda b, h, i: (b, heads + h)),
            pl.BlockSpec((s, HEAD_DIM), lambda b, h, i: (b, 2 * heads + h)),
            pl.BlockSpec((1, t, LANES), lambda b, h, i: (b, i, 0)),
            pl.BlockSpec((1, nq, t), lambda b, h, i: (b * LANES + h, 0, 0)),
        ],
        out_specs=pl.BlockSpec((t, HEAD_DIM), lambda b, h, i: (b * nq + i, h)),
        compiler_params=_params(3),
        name="fox_attention",
    )(qkv, qkv, qkv, cum, cum_rows)


def _mla_attn_kernel(q_ref, kn_ref, kr_ref, v_ref, o_ref, kcat_ref, *, t):
    qi = pl.program_id(2)

    @pl.when(qi == 0)
    def _():
        kcat_ref[:, :HEAD_DIM] = kn_ref[...]
        kcat_ref[:, HEAD_DIM:] = kr_ref[...]

    q = q_ref[...]

    def logits(j):
        start = pl.multiple_of(j * t, t)
        return _nt_dot(q, kcat_ref[pl.ds(start, t), :]), v_ref[pl.ds(start, t), :]

    def full_tile(j, carry):
        s, v = logits(j)
        return _online_softmax_step(s, v, carry)

    carry = lax.fori_loop(0, qi, full_tile, _softmax_init(t))
    s, v = logits(qi)
    row = lax.broadcasted_iota(jnp.int32, (t, t), 0)
    col = lax.broadcasted_iota(jnp.int32, (t, t), 1)
    s = jnp.where(col // CHUNK <= row // CHUNK, s, -jnp.inf)
    _, l, acc = _online_softmax_step(s, v, carry)
    o_ref[...] = (acc / l).astype(o_ref.dtype)


def _mla_attention(q_cat, kv, k_rope, batch, heads):
    m = q_cat.shape[0]
    s = m // batch
    t = _tile(s, 512)
    assert t % CHUNK == 0
    nq = s // t
    return pl.pallas_call(
        functools.partial(_mla_attn_kernel, t=t),
        out_shape=jax.ShapeDtypeStruct((m, heads * HEAD_DIM), BF16),
        grid=(batch, heads, nq),
        in_specs=[
            pl.BlockSpec((t, MLA_QK_PAD), lambda b, h, i: (b * nq + i, h)),
            pl.BlockSpec((s, HEAD_DIM), lambda b, h, i: (b, 2 * h)),
            pl.BlockSpec((s, LANES), lambda b, h, i: (b, 0)),
            pl.BlockSpec((s, HEAD_DIM), lambda b, h, i: (b, 2 * h + 1)),
        ],
        out_specs=pl.BlockSpec((t, HEAD_DIM), lambda b, h, i: (b * nq + i, h)),
        scratch_shapes=[pltpu.VMEM((s, MLA_QK_PAD), BF16)],
        compiler_params=_params(3),
        name="mla_attention",
    )(q_cat, kv, k_rope, kv)


def _rope_tables(s):
    half = MLA_ROPE_DIM // 2
    inv = ROPE_THETA ** (-jnp.arange(0, MLA_ROPE_DIM, 2, dtype=F32) / MLA_ROPE_DIM)
    ang = jnp.arange(s, dtype=F32)[:, None] * inv[None, :]
    cos, sin = jnp.cos(ang), jnp.sin(ang)
    zeros = jnp.zeros((s, LANES - 2 * half), F32)
    return (jnp.concatenate([cos, cos, zeros], axis=1), jnp.concatenate([-sin, sin, zeros], axis=1))


def _rope_slab(x, cos, sin):
    half = MLA_ROPE_DIM // 2
    lane = lax.broadcasted_iota(jnp.int32, x.shape, 1)
    swapped = jnp.where(lane < half, pltpu.roll(x, LANES - half, axis=1), pltpu.roll(x, half, axis=1))
    return x * cos + swapped * sin


def _dkv_kernel(x_ref, w_ref, g_ref, cos_ref, sin_ref, ckv_ref, kr_ref, *, lora):
    acc = jnp.dot(x_ref[...], w_ref[...], preferred_element_type=F32)
    c = acc[:, :lora]
    ckv_ref[...] = (c * lax.rsqrt(jnp.mean(c * c, axis=-1, keepdims=True) + RMS_EPS) * g_ref[...]).astype(BF16)
    kr_ref[...] = _rope_slab(acc[:, lora:], cos_ref[...], sin_ref[...]).astype(BF16)


def _mla_down_kv(xb, w_dkv, kv_norm, cos, sin, batch):
    m, d = xb.shape
    lora = kv_norm.shape[0]
    s = m // batch
    tm = _tile(s, 1024)
    ns = s // tm
    w = jnp.pad(w_dkv, ((0, 0), (0, LANES - MLA_ROPE_DIM))).astype(BF16)
    return pl.pallas_call(
        functools.partial(_dkv_kernel, lora=lora),
        out_shape=[jax.ShapeDtypeStruct((m, lora), BF16), jax.ShapeDtypeStruct((m, LANES), BF16)],
        grid=(m // tm,),
        in_specs=[
            pl.BlockSpec((tm, d), lambda i: (i, 0)),
            pl.BlockSpec((d, lora + LANES), lambda i: (0, 0)),
            pl.BlockSpec((1, lora), lambda i: (0, 0)),
            pl.BlockSpec((tm, LANES), lambda i: (i % ns, 0)),
            pl.BlockSpec((tm, LANES), lambda i: (i % ns, 0)),
        ],
        out_specs=[pl.BlockSpec((tm, lora), lambda i: (i, 0)), pl.BlockSpec((tm, LANES), lambda i: (i, 0))],
        compiler_params=_params(1),
        name="mla_down_kv",
    )(xb, w, kv_norm.reshape(1, lora), cos, sin)


def _dq_kernel(h_ref, w_ref, g_ref, o_ref):
    acc = jnp.dot(h_ref[...], w_ref[...], preferred_element_type=F32)
    o_ref[...] = (acc * lax.rsqrt(jnp.mean(acc * acc, axis=-1, keepdims=True) + RMS_EPS) * g_ref[...]).astype(BF16)


def _mla_down_q(h, w_dq, q_norm):
    m, d = h.shape
    lora = w_dq.shape[1]
    tm = _tile(m, 1024)
    return pl.pallas_call(
        _dq_kernel,
        out_shape=jax.ShapeDtypeStruct((m, lora), BF16),
        grid=(m // tm,),
        in_specs=[
            pl.BlockSpec((tm, d), lambda i: (i, 0)),
            pl.BlockSpec((d, lora), lambda i: (0, 0)),
            pl.BlockSpec((1, lora), lambda i: (0, 0)),
        ],
        out_specs=pl.BlockSpec((tm, lora), lambda i: (i, 0)),
        compiler_params=_params(1),
        name="mla_down_q",
    )(h, w_dq.astype(BF16), q_norm.reshape(1, lora))


def _uq_kernel(a_ref, w_ref, cos_ref, sin_ref, o_ref, *, scale, heads_per_tile):
    acc = jnp.dot(a_ref[...], w_ref[...], preferred_element_type=F32) * scale
    cos, sin = cos_ref[...], sin_ref[...]
    for g in range(heads_per_tile):
        lo = g * MLA_QK_PAD
        o_ref[:, lo:lo + HEAD_DIM] = acc[:, lo:lo + HEAD_DIM].astype(BF16)
        o_ref[:, lo + HEAD_DIM:lo + MLA_QK_PAD] = _rope_slab(
            acc[:, lo + HEAD_DIM:lo + MLA_QK_PAD], cos, sin).astype(BF16)


def _mla_up_q(qc, w_uq, cos, sin, batch, heads, scale):
    m, lora = qc.shape
    s = m // batch
    per_head = HEAD_DIM + MLA_ROPE_DIM
    w = jnp.pad(w_uq.reshape(lora, heads, per_head), ((0, 0), (0, 0), (0, MLA_QK_PAD - per_head)))
    w = w.reshape(lora, heads * MLA_QK_PAD).astype(BF16)
    n = heads * MLA_QK_PAD
    tm = _tile(s, 1024)
    ns = s // tm
    tn = _tile(n, 1024)
    return pl.pallas_call(
        functools.partial(_uq_kernel, scale=scale, heads_per_tile=tn // MLA_QK_PAD),
        out_shape=jax.ShapeDtypeStruct((m, n), BF16),
        grid=(m // tm, n // tn),
        in_specs=[
            pl.BlockSpec((tm, lora), lambda i, j: (i, 0)),
            pl.BlockSpec((lora, tn), lambda i, j: (0, j)),
            pl.BlockSpec((tm, LANES), lambda i, j: (i % ns, 0)),
            pl.BlockSpec((tm, LANES), lambda i, j: (i % ns, 0)),
        ],
        out_specs=pl.BlockSpec((tm, tn), lambda i, j: (i, j)),
        compiler_params=_params(2),
        name="mla_up_q",
    )(qc, w, cos, sin)


def _ffn_up_kernel(h_ref, wa_ref, wg_ref, cwa_ref, cwg_ref, cba_ref, cbg_ref, o_ref,
                   halo_a_ref, halo_g_ref, *, tm, tiles_per_seq):
    i = pl.program_id(0)
    j = pl.program_id(1)
    seq_start = (i % tiles_per_seq) == 0
    h = h_ref[...]
    row = lax.broadcasted_iota(jnp.int32, (tm, wa_ref.shape[1]), 0)

    def causal_conv(w_ref, cw_ref, cb_ref, halo_ref):
        u = jnp.dot(h, w_ref[...], preferred_element_type=F32)
        prev = jnp.where(seq_start, 0.0, halo_ref[j])
        halo_ref[j] = u[tm - SUBLANES:, :]
        p1 = prev[SUBLANES - 1:SUBLANES, :]
        p2 = prev[SUBLANES - 2:SUBLANES - 1, :]
        u1 = jnp.where(row == 0, p1, pltpu.roll(u, 1, axis=0))
        u2 = jnp.where(row == 0, p2, jnp.where(row == 1, p1, pltpu.roll(u, 2, axis=0)))
        cw = cw_ref[...]
        return cb_ref[...] + cw[0:1, :] * u2 + cw[1:2, :] * u1 + cw[2:3, :] * u

    a = causal_conv(wa_ref, cwa_ref, cba_ref, halo_a_ref)
    g = causal_conv(wg_ref, cwg_ref, cbg_ref, halo_g_ref)
    o_ref[...] = (g * jax.nn.sigmoid(g) * a).astype(o_ref.dtype)


def _ffn_up(h, w_up, conv_w, conv_b, batch):
    m, d = h.shape
    ff = w_up.shape[1] // 2
    s = m // batch
    tm = _tile(s, 1024)
    tn = _tile(ff, 256)
    nf = ff // tn
    a_col = lambda i, j: (0, j)
    g_col = lambda i, j: (0, nf + j)
    cb = conv_b.reshape(1, 2 * ff)
    return pl.pallas_call(
        functools.partial(_ffn_up_kernel, tm=tm, tiles_per_seq=s // tm),
        out_shape=jax.ShapeDtypeStruct((m, ff), BF16),
        grid=(m // tm, nf),
        in_specs=[
            pl.BlockSpec((tm, d), lambda i, j: (i, 0)),
            pl.BlockSpec((d, tn), a_col),
            pl.BlockSpec((d, tn), g_col),
            pl.BlockSpec((CONV_WIDTH, tn), a_col),
            pl.BlockSpec((CONV_WIDTH, tn), g_col),
            pl.BlockSpec((1, tn), a_col),
            pl.BlockSpec((1, tn), g_col),
        ],
        out_specs=pl.BlockSpec((tm, tn), lambda i, j: (i, j)),
        scratch_shapes=[pltpu.VMEM((nf, SUBLANES, tn), F32), pltpu.VMEM((nf, SUBLANES, tn), F32)],
        compiler_params=_params(2),
        name="ffn_up_conv_gate",
    )(h, w_up, w_up, conv_w, conv_w, cb, cb)


def _conv_ffn(h, w_up, conv_w, conv_b, w_down, batch):
    act = _ffn_up(h, w_up.astype(BF16), conv_w, conv_b, batch)
    return _mm(act, w_down.astype(BF16), F32, 512, 512, name="ffn_down")


def kernel(x, c, ada_w, ada_b, ln_g, ln_b, fox_w_qkv, fox_w_f, fox_b_f, fox_w_o, mla_w_dq, mla_q_norm,
           mla_w_uq, mla_w_o, mla_w_dkv, mla_kv_norm, mla_w_ukv, ffn_w_up, ffn_conv_w, ffn_conv_b, ffn_w_down):
    batch, seq, d = x.shape
    depth = ada_w.shape[0]
    n_a_layers = fox_w_qkv.shape[0]
    heads = d // HEAD_DIM
    alpha = (2.0 * depth) ** 0.25
    fox_scale = HEAD_DIM ** -0.5
    mla_scale = (HEAD_DIM + MLA_ROPE_DIM) ** -0.5

    ada = _ada_all(c, ada_w, ada_b)
    mods = [_split_ada(ada[lm], d) for lm in range(depth * 2)]
    cos, sin = _rope_tables(seq)

    x2d = x.reshape(batch * seq, d)
    shift, scale, _ = mods[0]
    h = _modulate(x2d, scale, shift, batch)
    kv = k_rope = None
    for layer in range(depth):
        gate_mix = mods[2 * layer][2]
        shift_ffn, scale_ffn, gate_ffn = mods[2 * layer + 1]
        if layer < n_a_layers:
            a = layer
            qkv = _mm(h, fox_w_qkv[a].astype(BF16), BF16, 1024, 1024,
                      scale=fox_scale, n_scaled_cols=heads * HEAD_DIM, name="fox_qkv")
            cum, cum_t = _forget_cumsum(h, fox_w_f[a], fox_b_f[a], batch)
            attn = _fox_attention(qkv, cum, cum_t, batch, heads)
            mix = _mm(attn, fox_w_o[a].astype(BF16), F32, 1024, 1024, name="fox_out")
        else:
            j = layer - n_a_layers
            qc = _mla_down_q(h, mla_w_dq[j], mla_q_norm[j])
            q_cat = _mla_up_q(qc, mla_w_uq[j], cos, sin, batch, heads, mla_scale)
            attn = _mla_attention(q_cat, kv, k_rope, batch, heads)
            mix = _mm(attn, mla_w_o[j].astype(BF16), F32, 1024, 1024, name="mla_out")
        x2d, h = _residual_ln(x2d, mix, gate_mix, ln_g[layer, 0], ln_b[layer, 0], batch, alpha,
                              next_mod=(scale_ffn, shift_ffn))
        ffn = _conv_ffn(h, ffn_w_up[layer], ffn_conv_w[layer], ffn_conv_b[layer], ffn_w_down[layer], batch)
        if layer + 1 < depth:
            shift_n, scale_n, _ = mods[2 * layer + 2]
            shares_kv = layer + 1 == n_a_layers
            outs = _residual_ln(x2d, ffn, gate_ffn, ln_g[layer, 1], ln_b[layer, 1], batch, alpha,
                                next_mod=(scale_n, shift_n), with_xb=shares_kv)
            x2d, h = outs[0], outs[1]
            if shares_kv:
                c_kv, k_rope = _mla_down_kv(outs[2], mla_w_dkv, mla_kv_norm, cos, sin, batch)
                kv = _mm(c_kv, mla_w_ukv.astype(BF16), BF16, 1024, 1024, name="mla_up_kv")
        else:
            (x2d,) = _residual_ln(x2d, ffn, gate_ffn, ln_g[layer, 1], ln_b[layer, 1], batch, alpha)
    return x2d.reshape(batch, seq, d)
```

```python
import functools
import math

import jax
import jax.numpy as jnp
from jax import lax
from jax.experimental import pallas as pl
from jax.experimental.pallas import tpu as pltpu

F32 = jnp.float32
BF16 = jnp.bfloat16

HEAD_DIM = 128
MLA_ROPE_DIM = 64
MLA_QK_PAD = 256
CHUNK = 64
CONV_WIDTH = 3
ROPE_THETA = 10000.0
LN_EPS = 1e-5
RMS_EPS = 1e-6
LOG2E = math.log2(math.e)
LANES = 128
SUBLANES = 8
VMEM_LIMIT_BYTES = 56 * 1024 * 1024
ATTN_TQ = 1024
ATTN_TK = 512
N_BIAS_PIECES = 3


def _params(n_grid_dims):
    return pltpu.CompilerParams(
        dimension_semantics=("arbitrary",) * n_grid_dims,
        vmem_limit_bytes=VMEM_LIMIT_BYTES)


def _tile(n, want):
    t = min(n, want)
    while n % t:
        t //= 2
    return t


def _ada_kernel(c_ref, w_ref, b_ref, o_ref):
    c = c_ref[...]
    c_act = (c * jax.nn.sigmoid(c)).astype(BF16)
    o_ref[0] = jnp.dot(c_act, w_ref[0, 0].astype(BF16), preferred_element_type=F32) + b_ref[0]


def _ada_all(c, ada_w, ada_b):
    depth, two, d, d3 = ada_w.shape
    b = c.shape[0]
    c_pad = jnp.pad(c, ((0, SUBLANES - b), (0, 0)))
    tn = _tile(d3, 512)
    out = pl.pallas_call(
        _ada_kernel,
        out_shape=jax.ShapeDtypeStruct((depth * two, SUBLANES, d3), F32),
        grid=(depth * two, d3 // tn),
        in_specs=[
            pl.BlockSpec((SUBLANES, d), lambda lm, j: (0, 0)),
            pl.BlockSpec((1, 1, d, tn), lambda lm, j: (lm // 2, lm % 2, 0, j)),
            pl.BlockSpec((1, 1, tn), lambda lm, j: (lm, 0, j)),
        ],
        out_specs=pl.BlockSpec((1, SUBLANES, tn), lambda lm, j: (lm, 0, j)),
        compiler_params=_params(2),
        name="ada_ln",
    )(c_pad, ada_w, ada_b.reshape(depth * two, 1, d3))
    return out[:, :b]


def _split_ada(ada_lm, d):
    return (ada_lm[:, None, 0:d], ada_lm[:, None, d:2 * d], ada_lm[:, None, 2 * d:3 * d])


def _modulate_kernel(x_ref, scale_ref, shift_ref, h_ref):
    h_ref[...] = (x_ref[...] * (1.0 + scale_ref[0]) + shift_ref[0]).astype(h_ref.dtype)


def _modulate(x2d, scale, shift, batch):
    m, d = x2d.shape
    s = m // batch
    ts = _tile(s, 512)
    ns = s // ts
    return pl.pallas_call(
        _modulate_kernel,
        out_shape=jax.ShapeDtypeStruct((m, d), BF16),
        grid=(batch, ns),
        in_specs=[
            pl.BlockSpec((ts, d), lambda b, i: (b * ns + i, 0)),
            pl.BlockSpec((1, 1, d), lambda b, i: (b, 0, 0)),
            pl.BlockSpec((1, 1, d), lambda b, i: (b, 0, 0)),
        ],
        out_specs=pl.BlockSpec((ts, d), lambda b, i: (b * ns + i, 0)),
        compiler_params=_params(2),
        name="modulate",
    )(x2d, scale, shift)


def _ln_kernel(*refs, alpha, with_h, with_xb):
    x_ref, mix_ref, gate_ref, g_ref, b_ref = refs[:5]
    refs = refs[5:]
    if with_h:
        scale_ref, shift_ref = refs[:2]
        refs = refs[2:]
    xo_ref = refs[0]
    refs = refs[1:]
    y = alpha * x_ref[...] + (1.0 + gate_ref[0]) * mix_ref[...]
    mu = jnp.mean(y, axis=-1, keepdims=True)
    dlt = y - mu
    var = jnp.mean(dlt * dlt, axis=-1, keepdims=True)
    xn = dlt * lax.rsqrt(var + LN_EPS) * g_ref[...] + b_ref[...]
    xo_ref[...] = xn
    if with_h:
        refs[0][...] = (xn * (1.0 + scale_ref[0]) + shift_ref[0]).astype(BF16)
        refs = refs[1:]
    if with_xb:
        refs[0][...] = xn.astype(BF16)


def _residual_ln(x2d, mix, gate, ln_g, ln_b, batch, alpha, next_mod=None, with_xb=False):
    m, d = x2d.shape
    s = m // batch
    ts = _tile(s, 256)
    ns = s // ts
    row = pl.BlockSpec((ts, d), lambda b, i: (b * ns + i, 0))
    per_batch = pl.BlockSpec((1, 1, d), lambda b, i: (b, 0, 0))
    vec = pl.BlockSpec((1, d), lambda b, i: (0, 0))
    with_h = next_mod is not None
    args = [x2d, mix, gate, ln_g.reshape(1, d), ln_b.reshape(1, d)]
    in_specs = [row, row, per_batch, vec, vec]
    out_shape = [jax.ShapeDtypeStruct((m, d), F32)]
    out_specs = [row]
    if with_h:
        args += [next_mod[0], next_mod[1]]
        in_specs += [per_batch, per_batch]
        out_shape.append(jax.ShapeDtypeStruct((m, d), BF16))
        out_specs.append(row)
    if with_xb:
        out_shape.append(jax.ShapeDtypeStruct((m, d), BF16))
        out_specs.append(row)
    return pl.pallas_call(
        functools.partial(_ln_kernel, alpha=alpha, with_h=with_h, with_xb=with_xb),
        out_shape=out_shape,
        grid=(batch, ns),
        in_specs=in_specs,
        out_specs=out_specs,
        compiler_params=_params(2),
        name="residual_ln",
    )(*args)


def _mm_kernel(a_ref, w_ref, o_ref, *, scale, n_scaled_tiles):
    acc = jnp.dot(a_ref[...], w_ref[...], preferred_element_type=F32)
    if n_scaled_tiles:
        acc = acc * jnp.where(pl.program_id(1) < n_scaled_tiles, scale, 1.0)
    o_ref[...] = acc.astype(o_ref.dtype)


def _mm(a, w, out_dtype, tm, tn, scale=1.0, n_scaled_cols=0, name="mm"):
    m, k = a.shape
    n = w.shape[1]
    tm, tn = _tile(m, tm), _tile(n, tn)
    if n_scaled_cols:
        tn = _tile(n_scaled_cols, tn)
        assert n % tn == 0
    return pl.pallas_call(
        functools.partial(_mm_kernel, scale=scale, n_scaled_tiles=n_scaled_cols // tn),
        out_shape=jax.ShapeDtypeStruct((m, n), out_dtype),
        grid=(m // tm, n // tn),
        in_specs=[
            pl.BlockSpec((tm, k), lambda i, j: (i, 0)),
            pl.BlockSpec((k, tn), lambda i, j: (0, j)),
        ],
        out_specs=pl.BlockSpec((tm, tn), lambda i, j: (i, j)),
        compiler_params=_params(2),
        name=name,
    )(a, w)


def _split3(x):
    hi = x.astype(BF16)
    r1 = x - hi.astype(F32)
    mid = r1.astype(BF16)
    lo = (r1 - mid.astype(F32)).astype(BF16)
    return hi, mid, lo


def _fgate_kernel(h_ref, wf_ref, bf_ref, pq_ref, pk_ref, oneq_ref, onek_ref, qaug_ref, kaug_ref, carry_ref, *, tm):
    @pl.when(pl.program_id(1) == 0)
    def _():
        carry_ref[...] = jnp.zeros_like(carry_ref)

    f = jnp.dot(h_ref[...], wf_ref[...], preferred_element_type=F32) + bf_ref[...]
    log_f = jnp.minimum(f, 0.0) - jnp.log1p(jnp.exp(-jnp.abs(f)))
    row = lax.broadcasted_iota(jnp.int32, (tm, tm), 0)
    col = lax.broadcasted_iota(jnp.int32, (tm, tm), 1)
    tril = jnp.where(col <= row, 1.0, 0.0).astype(BF16)
    cum = sum(jnp.dot(tril, piece, preferred_element_type=F32) for piece in _split3(log_f)) + carry_ref[...]
    carry_ref[...] = cum[tm - 1:tm, :]
    pieces = jnp.concatenate(_split3(cum * LOG2E), axis=1)
    qaug_ref[...] = (jnp.dot(pieces, pq_ref[...], preferred_element_type=F32) + oneq_ref[...]).astype(BF16)
    kaug_ref[...] = (jnp.dot(pieces, pk_ref[...], preferred_element_type=F32) + onek_ref[...]).astype(BF16)


def _forget_bias_columns(h, w_f, b_f, batch):
    m, d = h.shape
    s = m // batch
    heads = w_f.shape[1]
    tm = _tile(s, 256)
    ns = s // tm
    n = heads * LANES
    wf = jnp.pad(w_f, ((0, 0), (0, LANES - heads))).astype(BF16)
    bf = jnp.pad(b_f, (0, LANES - heads)).reshape(1, LANES)
    src = jnp.arange(N_BIAS_PIECES * LANES)[:, None]
    dst = jnp.arange(n)[None, :]
    same_head = (dst // LANES) == (src % LANES)
    piece = src // LANES
    pq = (same_head & (dst % LANES == piece)).astype(BF16)
    pk = -((same_head & (dst % LANES == piece + N_BIAS_PIECES)).astype(BF16))
    lane = dst % LANES
    oneq = ((lane >= N_BIAS_PIECES) & (lane < 2 * N_BIAS_PIECES)).astype(F32)
    onek = (lane < N_BIAS_PIECES).astype(F32)
    const = lambda shape: pl.BlockSpec(shape, lambda b, i: (0, 0))
    out_spec = pl.BlockSpec((tm, n), lambda b, i: (b * ns + i, 0))
    return pl.pallas_call(
        functools.partial(_fgate_kernel, tm=tm),
        out_shape=[jax.ShapeDtypeStruct((m, n), BF16), jax.ShapeDtypeStruct((m, n), BF16)],
        grid=(batch, ns),
        in_specs=[
            pl.BlockSpec((tm, d), lambda b, i: (b * ns + i, 0)),
            const((d, LANES)), const((1, LANES)),
            const((N_BIAS_PIECES * LANES, n)), const((N_BIAS_PIECES * LANES, n)),
            const((1, n)), const((1, n)),
        ],
        out_specs=[out_spec, out_spec],
        scratch_shapes=[pltpu.VMEM((1, LANES), F32)],
        compiler_params=_params(2),
        name="forget_bias_columns",
    )(h, wf, bf, pq, pk, oneq, onek)


def _attn_kernel(*refs, n_q_parts, tq, tk, chunk):
    q_refs = refs[:n_q_parts]
    k_ref, ka_ref, v_ref, o_ref, s0_ref, s1_ref, m_ref, l_ref, acc_ref = refs[n_q_parts:]
    qi = pl.program_id(2)
    q = q_refs[0][...] if n_q_parts == 1 else jnp.concatenate([r[...] for r in q_refs], axis=1)

    def qk(j, s_ref):
        start = pl.multiple_of(j * tk, tk)
        kk = jnp.concatenate([k_ref[pl.ds(start, tk), :], ka_ref[pl.ds(start, tk), :]], axis=1)
        s_ref[...] = lax.dot_general(kk, q, (((1,), (1,)), ((), ())), preferred_element_type=F32)

    def update(s, j):
        start = pl.multiple_of(j * tk, tk)
        v = v_ref[pl.ds(start, tk), :]
        m = m_ref[...]
        m_new = jnp.maximum(m, jnp.max(s, axis=0, keepdims=True))
        alpha = jnp.exp2(m - m_new)
        p = jnp.exp2(s - m_new)
        l_ref[...] = alpha * l_ref[...] + jnp.sum(p, axis=0, keepdims=True)
        pv = lax.dot_general(v, p.astype(BF16), (((0,), (0,)), ((), ())), preferred_element_type=F32)
        acc_ref[...] = alpha * acc_ref[...] + pv
        m_ref[...] = m_new

    m_ref[...] = jnp.full_like(m_ref, -jnp.inf)
    l_ref[...] = jnp.zeros_like(l_ref)
    acc_ref[...] = jnp.zeros_like(acc_ref)
    qk(0, s0_ref)

    def body(i, _):
        qk(2 * i + 1, s1_ref)
        update(s0_ref[...], 2 * i)
        qk(2 * i + 2, s0_ref)
        update(s1_ref[...], 2 * i + 1)
        return 0

    lax.fori_loop(0, qi, body, 0)
    key = lax.broadcasted_iota(jnp.int32, (tk, tq), 0) // chunk
    qry = lax.broadcasted_iota(jnp.int32, (tk, tq), 1) // chunk
    qk(2 * qi + 1, s1_ref)
    update(jnp.where(key <= qry, s0_ref[...], -jnp.inf), 2 * qi)
    update(jnp.where(key + tk // chunk <= qry, s1_ref[...], -jnp.inf), 2 * qi + 1)
    o_ref[...] = (acc_ref[...] / l_ref[...]).T.astype(o_ref.dtype)


def _attention(q_args, q_specs, k_args, k_specs, m, s, heads, chunk, name):
    tq, tk = ATTN_TQ, ATTN_TK
    assert s % tq == 0 and tq == 2 * tk and tk % chunk == 0
    nq = s // tq
    batch = m // s
    return pl.pallas_call(
        functools.partial(_attn_kernel, n_q_parts=len(q_args), tq=tq, tk=tk, chunk=chunk),
        out_shape=jax.ShapeDtypeStruct((m, heads * HEAD_DIM), BF16),
        grid=(batch, heads, nq),
        in_specs=list(q_specs) + list(k_specs),
        out_specs=pl.BlockSpec((tq, HEAD_DIM), lambda b, h, i: (b * nq + i, h)),
        scratch_shapes=[pltpu.VMEM((tk, tq), F32), pltpu.VMEM((tk, tq), F32),
                        pltpu.VMEM((1, tq), F32), pltpu.VMEM((1, tq), F32), pltpu.VMEM((HEAD_DIM, tq), F32)],
        compiler_params=_params(3),
        name=name,
    )(*q_args, *k_args)


def _fox_attention(qkv, qaug, kaug, batch, heads):
    m = qkv.shape[0]
    s = m // batch
    nq = s // ATTN_TQ
    q_tile = lambda shift: pl.BlockSpec((ATTN_TQ, HEAD_DIM), lambda b, h, i: (b * nq + i, shift + h))
    seq = lambda shift: pl.BlockSpec((s, HEAD_DIM), lambda b, h, i: (b, shift + h))
    return _attention((qkv, qaug), (q_tile(0), q_tile(0)),
                      (qkv, kaug, qkv), (seq(heads), seq(0), seq(2 * heads)),
                      m, s, heads, 1, "fox_attention")


def _mla_attention(q_cat, kv, k_rope, batch, heads):
    m = q_cat.shape[0]
    s = m // batch
    nq = s // ATTN_TQ
    return _attention(
        (q_cat,), (pl.BlockSpec((ATTN_TQ, MLA_QK_PAD), lambda b, h, i: (b * nq + i, h)),),
        (kv, k_rope, kv),
        (pl.BlockSpec((s, HEAD_DIM), lambda b, h, i: (b, 2 * h)),
         pl.BlockSpec((s, LANES), lambda b, h, i: (b, 0)),
         pl.BlockSpec((s, HEAD_DIM), lambda b, h, i: (b, 2 * h + 1))),
        m, s, heads, CHUNK, "mla_attention")


def _rope_tables(s):
    half = MLA_ROPE_DIM // 2
    inv = ROPE_THETA ** (-jnp.arange(0, MLA_ROPE_DIM, 2, dtype=F32) / MLA_ROPE_DIM)
    ang = jnp.arange(s, dtype=F32)[:, None] * inv[None, :]
    cos, sin = jnp.cos(ang), jnp.sin(ang)
    zeros = jnp.zeros((s, LANES - 2 * half), F32)
    return (jnp.concatenate([cos, cos, zeros], axis=1), jnp.concatenate([-sin, sin, zeros], axis=1))


def _rope_slab(x, cos, sin):
    half = MLA_ROPE_DIM // 2
    lane = lax.broadcasted_iota(jnp.int32, x.shape, 1)
    swapped = jnp.where(lane < half, pltpu.roll(x, LANES - half, axis=1), pltpu.roll(x, half, axis=1))
    return x * cos + swapped * sin


def _dkv_kernel(x_ref, w_ref, g_ref, cos_ref, sin_ref, ckv_ref, kr_ref, *, lora):
    acc = jnp.dot(x_ref[...], w_ref[...], preferred_element_type=F32)
    c = acc[:, :lora]
    ckv_ref[...] = (c * lax.rsqrt(jnp.mean(c * c, axis=-1, keepdims=True) + RMS_EPS) * g_ref[...]).astype(BF16)
    kr_ref[...] = _rope_slab(acc[:, lora:], cos_ref[...], sin_ref[...]).astype(BF16)


def _mla_down_kv(xb, w_dkv, kv_norm, cos, sin, batch):
    m, d = xb.shape
    lora = kv_norm.shape[0]
    s = m // batch
    tm = _tile(s, 1024)
    ns = s // tm
    w = jnp.pad(w_dkv, ((0, 0), (0, LANES - MLA_ROPE_DIM))).astype(BF16)
    return pl.pallas_call(
        functools.partial(_dkv_kernel, lora=lora),
        out_shape=[jax.ShapeDtypeStruct((m, lora), BF16), jax.ShapeDtypeStruct((m, LANES), BF16)],
        grid=(m // tm,),
        in_specs=[
            pl.BlockSpec((tm, d), lambda i: (i, 0)),
            pl.BlockSpec((d, lora + LANES), lambda i: (0, 0)),
            pl.BlockSpec((1, lora), lambda i: (0, 0)),
            pl.BlockSpec((tm, LANES), lambda i: (i % ns, 0)),
            pl.BlockSpec((tm, LANES), lambda i: (i % ns, 0)),
        ],
        out_specs=[pl.BlockSpec((tm, lora), lambda i: (i, 0)), pl.BlockSpec((tm, LANES), lambda i: (i, 0))],
        compiler_params=_params(1),
        name="mla_down_kv",
    )(xb, w, kv_norm.reshape(1, lora), cos, sin)


def _dq_kernel(h_ref, w_ref, g_ref, o_ref):
    acc = jnp.dot(h_ref[...], w_ref[...], preferred_element_type=F32)
    o_ref[...] = (acc * lax.rsqrt(jnp.mean(acc * acc, axis=-1, keepdims=True) + RMS_EPS) * g_ref[...]).astype(BF16)


def _mla_down_q(h, w_dq, q_norm):
    m, d = h.shape
    lora = w_dq.shape[1]
    tm = _tile(m, 1024)
    return pl.pallas_call(
        _dq_kernel,
        out_shape=jax.ShapeDtypeStruct((m, lora), BF16),
        grid=(m // tm,),
        in_specs=[
            pl.BlockSpec((tm, d), lambda i: (i, 0)),
            pl.BlockSpec((d, lora), lambda i: (0, 0)),
            pl.BlockSpec((1, lora), lambda i: (0, 0)),
        ],
        out_specs=pl.BlockSpec((tm, lora), lambda i: (i, 0)),
        compiler_params=_params(1),
        name="mla_down_q",
    )(h, w_dq.astype(BF16), q_norm.reshape(1, lora))


def _uq_kernel(a_ref, w_ref, cos_ref, sin_ref, o_ref, *, scale, heads_per_tile):
    acc = jnp.dot(a_ref[...], w_ref[...], preferred_element_type=F32) * scale
    cos, sin = cos_ref[...], sin_ref[...]
    for g in range(heads_per_tile):
        lo = g * MLA_QK_PAD
        o_ref[:, lo:lo + HEAD_DIM] = acc[:, lo:lo + HEAD_DIM].astype(BF16)
        o_ref[:, lo + HEAD_DIM:lo + MLA_QK_PAD] = _rope_slab(
            acc[:, lo + HEAD_DIM:lo + MLA_QK_PAD], cos, sin).astype(BF16)


def _mla_up_q(qc, w_uq, cos, sin, batch, heads, scale):
    m, lora = qc.shape
    s = m // batch
    per_head = HEAD_DIM + MLA_ROPE_DIM
    w = jnp.pad(w_uq.reshape(lora, heads, per_head), ((0, 0), (0, 0), (0, MLA_QK_PAD - per_head)))
    w = w.reshape(lora, heads * MLA_QK_PAD).astype(BF16)
    n = heads * MLA_QK_PAD
    tm = _tile(s, 1024)
    ns = s // tm
    tn = _tile(n, 1024)
    return pl.pallas_call(
        functools.partial(_uq_kernel, scale=scale, heads_per_tile=tn // MLA_QK_PAD),
        out_shape=jax.ShapeDtypeStruct((m, n), BF16),
        grid=(m // tm, n // tn),
        in_specs=[
            pl.BlockSpec((tm, lora), lambda i, j: (i, 0)),
            pl.BlockSpec((lora, tn), lambda i, j: (0, j)),
            pl.BlockSpec((tm, LANES), lambda i, j: (i % ns, 0)),
            pl.BlockSpec((tm, LANES), lambda i, j: (i % ns, 0)),
        ],
        out_specs=pl.BlockSpec((tm, tn), lambda i, j: (i, j)),
        compiler_params=_params(2),
        name="mla_up_q",
    )(qc, w, cos, sin)


def _ffn_up_kernel(h_ref, wa_ref, wg_ref, cwa_ref, cwg_ref, cba_ref, cbg_ref, o_ref,
                   halo_a_ref, halo_g_ref, *, tm, tiles_per_seq):
    i = pl.program_id(0)
    j = pl.program_id(1)
    seq_start = (i % tiles_per_seq) == 0
    h = h_ref[...]
    row = lax.broadcasted_iota(jnp.int32, (tm, wa_ref.shape[1]), 0)

    def causal_conv(w_ref, cw_ref, cb_ref, halo_ref):
        u = jnp.dot(h, w_ref[...], preferred_element_type=F32)
        prev = jnp.where(seq_start, 0.0, halo_ref[j])
        halo_ref[j] = u[tm - SUBLANES:, :]
        p1 = prev[SUBLANES - 1:SUBLANES, :]
        p2 = prev[SUBLANES - 2:SUBLANES - 1, :]
        u1 = jnp.where(row == 0, p1, pltpu.roll(u, 1, axis=0))
        u2 = jnp.where(row == 0, p2, jnp.where(row == 1, p1, pltpu.roll(u, 2, axis=0)))
        cw = cw_ref[...]
        return cb_ref[...] + cw[0:1, :] * u2 + cw[1:2, :] * u1 + cw[2:3, :] * u

    a = causal_conv(wa_ref, cwa_ref, cba_ref, halo_a_ref)
    g = causal_conv(wg_ref, cwg_ref, cbg_ref, halo_g_ref)
    o_ref[...] = (g * jax.nn.sigmoid(g) * a).astype(o_ref.dtype)


def _ffn_up(h, w_up, conv_w, conv_b, batch):
    m, d = h.shape
    ff = w_up.shape[1] // 2
    s = m // batch
    tm = _tile(s, 1024)
    tn = _tile(ff, 256)
    nf = ff // tn
    a_col = lambda i, j: (0, j)
    g_col = lambda i, j: (0, nf + j)
    cb = conv_b.reshape(1, 2 * ff)
    return pl.pallas_call(
        functools.partial(_ffn_up_kernel, tm=tm, tiles_per_seq=s // tm),
        out_shape=jax.ShapeDtypeStruct((m, ff), BF16),
        grid=(m // tm, nf),
        in_specs=[
            pl.BlockSpec((tm, d), lambda i, j: (i, 0)),
            pl.BlockSpec((d, tn), a_col),
            pl.BlockSpec((d, tn), g_col),
            pl.BlockSpec((CONV_WIDTH, tn), a_col),
            pl.BlockSpec((CONV_WIDTH, tn), g_col),
            pl.BlockSpec((1, tn), a_col),
            pl.BlockSpec((1, tn), g_col),
        ],
        out_specs=pl.BlockSpec((tm, tn), lambda i, j: (i, j)),
        scratch_shapes=[pltpu.VMEM((nf, SUBLANES, tn), F32), pltpu.VMEM((nf, SUBLANES, tn), F32)],
        compiler_params=_params(2),
        name="ffn_up_conv_gate",
    )(h, w_up, w_up, conv_w, conv_w, cb, cb)


def _conv_ffn(h, w_up, conv_w, conv_b, w_down, batch):
    act = _ffn_up(h, w_up.astype(BF16), conv_w, conv_b, batch)
    return _mm(act, w_down.astype(BF16), F32, 512, 512, name="ffn_down")


def kernel(x, c, ada_w, ada_b, ln_g, ln_b, fox_w_qkv, fox_w_f, fox_b_f, fox_w_o, mla_w_dq, mla_q_norm,
           mla_w_uq, mla_w_o, mla_w_dkv, mla_kv_norm, mla_w_ukv, ffn_w_up, ffn_conv_w, ffn_conv_b, ffn_w_down):
    batch, seq, d = x.shape
    depth = ada_w.shape[0]
    n_a_layers = fox_w_qkv.shape[0]
    heads = d // HEAD_DIM
    alpha = (2.0 * depth) ** 0.25
    fox_scale = LOG2E * HEAD_DIM ** -0.5
    mla_scale = LOG2E * (HEAD_DIM + MLA_ROPE_DIM) ** -0.5

    ada = _ada_all(c, ada_w, ada_b)
    mods = [_split_ada(ada[lm], d) for lm in range(depth * 2)]
    cos, sin = _rope_tables(seq)

    x2d = x.reshape(batch * seq, d)
    shift, scale, _ = mods[0]
    h = _modulate(x2d, scale, shift, batch)
    kv = k_rope = None
    for layer in range(depth):
        gate_mix = mods[2 * layer][2]
        shift_ffn, scale_ffn, gate_ffn = mods[2 * layer + 1]
        if layer < n_a_layers:
            a = layer
            qkv = _mm(h, fox_w_qkv[a].astype(BF16), BF16, 1024, 1024,
                      scale=fox_scale, n_scaled_cols=heads * HEAD_DIM, name="fox_qkv")
            qaug, kaug = _forget_bias_columns(h, fox_w_f[a], fox_b_f[a], batch)
            attn = _fox_attention(qkv, qaug, kaug, batch, heads)
            mix = _mm(attn, fox_w_o[a].astype(BF16), F32, 1024, 1024, name="fox_out")
        else:
            j = layer - n_a_layers
            qc = _mla_down_q(h, mla_w_dq[j], mla_q_norm[j])
            q_cat = _mla_up_q(qc, mla_w_uq[j], cos, sin, batch, heads, mla_scale)
            attn = _mla_attention(q_cat, kv, k_rope, batch, heads)
            mix = _mm(attn, mla_w_o[j].astype(BF16), F32, 1024, 1024, name="mla_out")
        x2d, h = _residual_ln(x2d, mix, gate_mix, ln_g[layer, 0], ln_b[layer, 0], batch, alpha,
                              next_mod=(scale_ffn, shift_ffn))
        ffn = _conv_ffn(h, ffn_w_up[layer], ffn_conv_w[layer], ffn_conv_b[layer], ffn_w_down[layer], batch)
        if layer + 1 < depth:
            shift_n, scale_n, _ = mods[2 * layer + 2]
            shares_kv = layer + 1 == n_a_layers
            outs = _residual_ln(x2d, ffn, gate_ffn, ln_g[layer, 1], ln_b[layer, 1], batch, alpha,
                                next_mod=(scale_n, shift_n), with_xb=shares_kv)
            x2d, h = outs[0], outs[1]
            if shares_kv:
                c_kv, k_rope = _mla_down_kv(outs[2], mla_w_dkv, mla_kv_norm, cos, sin, batch)
                kv = _mm(c_kv, mla_w_ukv.astype(BF16), BF16, 1024, 1024, name="mla_up_kv")
        else:
            (x2d,) = _residual_ln(x2d, ffn, gate_ffn, ln_g[layer, 1], ln_b[layer, 1], batch, alpha)
    return x2d.reshape(batch, seq, d)
```

```python
import functools
import math

import jax
import jax.numpy as jnp
from jax import lax
from jax.experimental import pallas as pl
from jax.experimental.pallas import tpu as pltpu

F32 = jnp.float32
BF16 = jnp.bfloat16

HEAD_DIM = 128
MLA_ROPE_DIM = 64
MLA_QK_PAD = 256
CHUNK = 64
CONV_WIDTH = 3
ROPE_THETA = 10000.0
LN_EPS = 1e-5
RMS_EPS = 1e-6
LOG2E = math.log2(math.e)
LANES = 128
SUBLANES = 8
VMEM_LIMIT_BYTES = 56 * 1024 * 1024
ATTN_TQ = 1024
ATTN_TK = 512
HEADS_PER_STEP = 2
N_BIAS_PIECES = 3


def _params(n_grid_dims):
    return pltpu.CompilerParams(
        dimension_semantics=("arbitrary",) * n_grid_dims,
        vmem_limit_bytes=VMEM_LIMIT_BYTES)


def _tile(n, want):
    t = min(n, want)
    while n % t:
        t //= 2
    return t


def _ada_kernel(c_ref, w_ref, b_ref, o_ref):
    c = c_ref[...]
    c_act = (c * jax.nn.sigmoid(c)).astype(BF16)
    o_ref[0] = jnp.dot(c_act, w_ref[0, 0].astype(BF16), preferred_element_type=F32) + b_ref[0]


def _ada_all(c, ada_w, ada_b):
    depth, two, d, d3 = ada_w.shape
    b = c.shape[0]
    c_pad = jnp.pad(c, ((0, SUBLANES - b), (0, 0)))
    tn = _tile(d3, 512)
    out = pl.pallas_call(
        _ada_kernel,
        out_shape=jax.ShapeDtypeStruct((depth * two, SUBLANES, d3), F32),
        grid=(depth * two, d3 // tn),
        in_specs=[
            pl.BlockSpec((SUBLANES, d), lambda lm, j: (0, 0)),
            pl.BlockSpec((1, 1, d, tn), lambda lm, j: (lm // 2, lm % 2, 0, j)),
            pl.BlockSpec((1, 1, tn), lambda lm, j: (lm, 0, j)),
        ],
        out_specs=pl.BlockSpec((1, SUBLANES, tn), lambda lm, j: (lm, 0, j)),
        compiler_params=_params(2),
        name="ada_ln",
    )(c_pad, ada_w, ada_b.reshape(depth * two, 1, d3))
    return out[:, :b]


def _split_ada(ada_lm, d):
    return (ada_lm[:, None, 0:d], ada_lm[:, None, d:2 * d], ada_lm[:, None, 2 * d:3 * d])


def _modulate_kernel(x_ref, scale_ref, shift_ref, h_ref):
    h_ref[...] = (x_ref[...] * (1.0 + scale_ref[0]) + shift_ref[0]).astype(h_ref.dtype)


def _modulate(x2d, scale, shift, batch):
    m, d = x2d.shape
    s = m // batch
    ts = _tile(s, 512)
    ns = s // ts
    return pl.pallas_call(
        _modulate_kernel,
        out_shape=jax.ShapeDtypeStruct((m, d), BF16),
        grid=(batch, ns),
        in_specs=[
            pl.BlockSpec((ts, d), lambda b, i: (b * ns + i, 0)),
            pl.BlockSpec((1, 1, d), lambda b, i: (b, 0, 0)),
            pl.BlockSpec((1, 1, d), lambda b, i: (b, 0, 0)),
        ],
        out_specs=pl.BlockSpec((ts, d), lambda b, i: (b * ns + i, 0)),
        compiler_params=_params(2),
        name="modulate",
    )(x2d, scale, shift)


def _ln_kernel(*refs, alpha, with_h, with_xb):
    x_ref, mix_ref, gate_ref, g_ref, b_ref = refs[:5]
    refs = refs[5:]
    if with_h:
        scale_ref, shift_ref = refs[:2]
        refs = refs[2:]
    xo_ref = refs[0]
    refs = refs[1:]
    y = alpha * x_ref[...] + (1.0 + gate_ref[0]) * mix_ref[...]
    mu = jnp.mean(y, axis=-1, keepdims=True)
    dlt = y - mu
    var = jnp.mean(dlt * dlt, axis=-1, keepdims=True)
    xn = dlt * lax.rsqrt(var + LN_EPS) * g_ref[...] + b_ref[...]
    xo_ref[...] = xn
    if with_h:
        refs[0][...] = (xn * (1.0 + scale_ref[0]) + shift_ref[0]).astype(BF16)
        refs = refs[1:]
    if with_xb:
        refs[0][...] = xn.astype(BF16)


def _residual_ln(x2d, mix, gate, ln_g, ln_b, batch, alpha, next_mod=None, with_xb=False):
    m, d = x2d.shape
    s = m // batch
    ts = _tile(s, 256)
    ns = s // ts
    row = pl.BlockSpec((ts, d), lambda b, i: (b * ns + i, 0))
    per_batch = pl.BlockSpec((1, 1, d), lambda b, i: (b, 0, 0))
    vec = pl.BlockSpec((1, d), lambda b, i: (0, 0))
    with_h = next_mod is not None
    args = [x2d, mix, gate, ln_g.reshape(1, d), ln_b.reshape(1, d)]
    in_specs = [row, row, per_batch, vec, vec]
    out_shape = [jax.ShapeDtypeStruct((m, d), F32)]
    out_specs = [row]
    if with_h:
        args += [next_mod[0], next_mod[1]]
        in_specs += [per_batch, per_batch]
        out_shape.append(jax.ShapeDtypeStruct((m, d), BF16))
        out_specs.append(row)
    if with_xb:
        out_shape.append(jax.ShapeDtypeStruct((m, d), BF16))
        out_specs.append(row)
    return pl.pallas_call(
        functools.partial(_ln_kernel, alpha=alpha, with_h=with_h, with_xb=with_xb),
        out_shape=out_shape,
        grid=(batch, ns),
        in_specs=in_specs,
        out_specs=out_specs,
        compiler_params=_params(2),
        name="residual_ln",
    )(*args)


def _mm_kernel(a_ref, w_ref, o_ref, *, scale, n_scaled_tiles):
    acc = jnp.dot(a_ref[...], w_ref[...], preferred_element_type=F32)
    if n_scaled_tiles:
        acc = acc * jnp.where(pl.program_id(1) < n_scaled_tiles, scale, 1.0)
    o_ref[...] = acc.astype(o_ref.dtype)


def _mm(a, w, out_dtype, tm, tn, scale=1.0, n_scaled_cols=0, name="mm"):
    m, k = a.shape
    n = w.shape[1]
    tm, tn = _tile(m, tm), _tile(n, tn)
    if n_scaled_cols:
        tn = _tile(n_scaled_cols, tn)
        assert n % tn == 0
    return pl.pallas_call(
        functools.partial(_mm_kernel, scale=scale, n_scaled_tiles=n_scaled_cols // tn),
        out_shape=jax.ShapeDtypeStruct((m, n), out_dtype),
        grid=(m // tm, n // tn),
        in_specs=[
            pl.BlockSpec((tm, k), lambda i, j: (i, 0)),
            pl.BlockSpec((k, tn), lambda i, j: (0, j)),
        ],
        out_specs=pl.BlockSpec((tm, tn), lambda i, j: (i, j)),
        compiler_params=_params(2),
        name=name,
    )(a, w)


def _split3(x):
    hi = x.astype(BF16)
    r1 = x - hi.astype(F32)
    mid = r1.astype(BF16)
    lo = (r1 - mid.astype(F32)).astype(BF16)
    return hi, mid, lo


def _fgate_kernel(h_ref, wf_ref, bf_ref, pq_ref, pk_ref, oneq_ref, onek_ref, qaug_ref, kaug_ref, carry_ref, *, tm):
    @pl.when(pl.program_id(1) == 0)
    def _():
        carry_ref[...] = jnp.zeros_like(carry_ref)

    f = jnp.dot(h_ref[...], wf_ref[...], preferred_element_type=F32) + bf_ref[...]
    log_f = jnp.minimum(f, 0.0) - jnp.log1p(jnp.exp(-jnp.abs(f)))
    row = lax.broadcasted_iota(jnp.int32, (tm, tm), 0)
    col = lax.broadcasted_iota(jnp.int32, (tm, tm), 1)
    tril = jnp.where(col <= row, 1.0, 0.0).astype(BF16)
    cum = sum(jnp.dot(tril, piece, preferred_element_type=F32) for piece in _split3(log_f)) + carry_ref[...]
    carry_ref[...] = cum[tm - 1:tm, :]
    pieces = jnp.concatenate(_split3(cum * LOG2E), axis=1)
    qaug_ref[...] = (jnp.dot(pieces, pq_ref[...], preferred_element_type=F32) + oneq_ref[...]).astype(BF16)
    kaug_ref[...] = (jnp.dot(pieces, pk_ref[...], preferred_element_type=F32) + onek_ref[...]).astype(BF16)


def _forget_bias_columns(h, w_f, b_f, batch):
    m, d = h.shape
    s = m // batch
    heads = w_f.shape[1]
    tm = _tile(s, 256)
    ns = s // tm
    n = heads * LANES
    wf = jnp.pad(w_f, ((0, 0), (0, LANES - heads))).astype(BF16)
    bf = jnp.pad(b_f, (0, LANES - heads)).reshape(1, LANES)
    src = jnp.arange(N_BIAS_PIECES * LANES)[:, None]
    dst = jnp.arange(n)[None, :]
    same_head = (dst // LANES) == (src % LANES)
    piece = src // LANES
    pq = (same_head & (dst % LANES == piece)).astype(BF16)
    pk = -((same_head & (dst % LANES == piece + N_BIAS_PIECES)).astype(BF16))
    lane = dst % LANES
    oneq = ((lane >= N_BIAS_PIECES) & (lane < 2 * N_BIAS_PIECES)).astype(F32)
    onek = (lane < N_BIAS_PIECES).astype(F32)
    const = lambda shape: pl.BlockSpec(shape, lambda b, i: (0, 0))
    out_spec = pl.BlockSpec((tm, n), lambda b, i: (b * ns + i, 0))
    return pl.pallas_call(
        functools.partial(_fgate_kernel, tm=tm),
        out_shape=[jax.ShapeDtypeStruct((m, n), BF16), jax.ShapeDtypeStruct((m, n), BF16)],
        grid=(batch, ns),
        in_specs=[
            pl.BlockSpec((tm, d), lambda b, i: (b * ns + i, 0)),
            const((d, LANES)), const((1, LANES)),
            const((N_BIAS_PIECES * LANES, n)), const((N_BIAS_PIECES * LANES, n)),
            const((1, n)), const((1, n)),
        ],
        out_specs=[out_spec, out_spec],
        scratch_shapes=[pltpu.VMEM((1, LANES), F32)],
        compiler_params=_params(2),
        name="forget_bias_columns",
    )(h, wf, bf, pq, pk, oneq, onek)


def _attn_kernel(*refs, n_q_parts, tq, tk, chunk):
    n_in = n_q_parts + 3
    heads = []
    for hd in range(HEADS_PER_STEP):
        ins = refs[hd * n_in:(hd + 1) * n_in]
        scr = refs[HEADS_PER_STEP * n_in + 1 + hd * 5:HEADS_PER_STEP * n_in + 1 + (hd + 1) * 5]
        heads.append(dict(q=ins[:n_q_parts], k=ins[n_q_parts], ka=ins[n_q_parts + 1], v=ins[n_q_parts + 2],
                          s=scr[0:2], m=scr[2], l=scr[3], acc=scr[4]))
    o_ref = refs[HEADS_PER_STEP * n_in]
    qi = pl.program_id(2)

    def qk(hd, j, slot):
        q = hd["q"][0][...] if n_q_parts == 1 else jnp.concatenate([r[...] for r in hd["q"]], axis=1)
        start = pl.multiple_of(j * tk, tk)
        kk = jnp.concatenate([hd["k"][pl.ds(start, tk), :], hd["ka"][pl.ds(start, tk), :]], axis=1)
        hd["s"][slot][...] = lax.dot_general(kk, q, (((1,), (1,)), ((), ())), preferred_element_type=F32)

    def update(hd, s, j):
        start = pl.multiple_of(j * tk, tk)
        v = hd["v"][pl.ds(start, tk), :]
        m = hd["m"][...]
        m_new = jnp.maximum(m, jnp.max(s, axis=0, keepdims=True))
        alpha = jnp.exp2(m - m_new)
        p = jnp.exp2(s - m_new)
        hd["l"][...] = alpha * hd["l"][...] + jnp.sum(p, axis=0, keepdims=True)
        pv = lax.dot_general(v, p.astype(BF16), (((0,), (0,)), ((), ())), preferred_element_type=F32)
        hd["acc"][...] = alpha * hd["acc"][...] + pv
        hd["m"][...] = m_new

    for hd in heads:
        hd["m"][...] = jnp.full_like(hd["m"], -jnp.inf)
        hd["l"][...] = jnp.zeros_like(hd["l"])
        hd["acc"][...] = jnp.zeros_like(hd["acc"])
        qk(hd, 0, 0)

    def body(i, _):
        for slot in range(2):
            for hd in heads:
                qk(hd, 2 * i + 1 + slot, 1 - slot)
                update(hd, hd["s"][slot][...], 2 * i + slot)
        return 0

    lax.fori_loop(0, qi, body, 0)
    key = lax.broadcasted_iota(jnp.int32, (tk, tq), 0) // chunk
    qry = lax.broadcasted_iota(jnp.int32, (tk, tq), 1) // chunk
    for hd in heads:
        qk(hd, 2 * qi + 1, 1)
        update(hd, jnp.where(key <= qry, hd["s"][0][...], -jnp.inf), 2 * qi)
    for n, hd in enumerate(heads):
        update(hd, jnp.where(key + tk // chunk <= qry, hd["s"][1][...], -jnp.inf), 2 * qi + 1)
        o_ref[:, n * HEAD_DIM:(n + 1) * HEAD_DIM] = (hd["acc"][...] / hd["l"][...]).T.astype(o_ref.dtype)


def _attention(head_args, head_specs, n_q_parts, m, s, heads, chunk, name):
    tq, tk = ATTN_TQ, ATTN_TK
    assert s % tq == 0 and tq == 2 * tk and tk % chunk == 0 and heads % HEADS_PER_STEP == 0
    nq = s // tq
    batch = m // s
    per_head_scratch = [pltpu.VMEM((tk, tq), F32), pltpu.VMEM((tk, tq), F32),
                        pltpu.VMEM((1, tq), F32), pltpu.VMEM((1, tq), F32), pltpu.VMEM((HEAD_DIM, tq), F32)]
    return pl.pallas_call(
        functools.partial(_attn_kernel, n_q_parts=n_q_parts, tq=tq, tk=tk, chunk=chunk),
        out_shape=jax.ShapeDtypeStruct((m, heads * HEAD_DIM), BF16),
        grid=(batch, heads // HEADS_PER_STEP, nq),
        in_specs=[spec for par in range(HEADS_PER_STEP) for spec in head_specs(par)],
        out_specs=pl.BlockSpec((tq, HEADS_PER_STEP * HEAD_DIM), lambda b, g, i: (b * nq + i, g)),
        scratch_shapes=per_head_scratch * HEADS_PER_STEP,
        compiler_params=_params(3),
        name=name,
    )(*(list(head_args) * HEADS_PER_STEP))


def _fox_attention(qkv, qaug, kaug, batch, heads):
    m = qkv.shape[0]
    s = m // batch
    nq = s // ATTN_TQ

    def specs(par):
        head = lambda g: HEADS_PER_STEP * g + par
        q_tile = lambda shift: pl.BlockSpec((ATTN_TQ, HEAD_DIM), lambda b, g, i: (b * nq + i, shift + head(g)))
        seq = lambda shift: pl.BlockSpec((s, HEAD_DIM), lambda b, g, i: (b, shift + head(g)))
        return [q_tile(0), q_tile(0), seq(heads), seq(0), seq(2 * heads)]

    return _attention((qkv, qaug, qkv, kaug, qkv), specs, 2, m, s, heads, 1, "fox_attention")


def _mla_attention(q_cat, kv, k_rope, batch, heads):
    m = q_cat.shape[0]
    s = m // batch
    nq = s // ATTN_TQ

    def specs(par):
        head = lambda g: HEADS_PER_STEP * g + par
        return [pl.BlockSpec((ATTN_TQ, MLA_QK_PAD), lambda b, g, i: (b * nq + i, head(g))),
                pl.BlockSpec((s, HEAD_DIM), lambda b, g, i: (b, 2 * head(g))),
                pl.BlockSpec((s, LANES), lambda b, g, i: (b, 0)),
                pl.BlockSpec((s, HEAD_DIM), lambda b, g, i: (b, 2 * head(g) + 1))]

    return _attention((q_cat, kv, k_rope, kv), specs, 1, m, s, heads, CHUNK, "mla_attention")


def _rope_tables(s):
    half = MLA_ROPE_DIM // 2
    inv = ROPE_THETA ** (-jnp.arange(0, MLA_ROPE_DIM, 2, dtype=F32) / MLA_ROPE_DIM)
    ang = jnp.arange(s, dtype=F32)[:, None] * inv[None, :]
    cos, sin = jnp.cos(ang), jnp.sin(ang)
    zeros = jnp.zeros((s, LANES - 2 * half), F32)
    return (jnp.concatenate([cos, cos, zeros], axis=1), jnp.concatenate([-sin, sin, zeros], axis=1))


def _rope_slab(x, cos, sin):
    half = MLA_ROPE_DIM // 2
    lane = lax.broadcasted_iota(jnp.int32, x.shape, 1)
    swapped = jnp.where(lane < half, pltpu.roll(x, LANES - half, axis=1), pltpu.roll(x, half, axis=1))
    return x * cos + swapped * sin


def _dkv_kernel(x_ref, w_ref, g_ref, cos_ref, sin_ref, ckv_ref, kr_ref, *, lora):
    acc = jnp.dot(x_ref[...], w_ref[...], preferred_element_type=F32)
    c = acc[:, :lora]
    ckv_ref[...] = (c * lax.rsqrt(jnp.mean(c * c, axis=-1, keepdims=True) + RMS_EPS) * g_ref[...]).astype(BF16)
    kr_ref[...] = _rope_slab(acc[:, lora:], cos_ref[...], sin_ref[...]).astype(BF16)


def _mla_down_kv(xb, w_dkv, kv_norm, cos, sin, batch):
    m, d = xb.shape
    lora = kv_norm.shape[0]
    s = m // batch
    tm = _tile(s, 1024)
    ns = s // tm
    w = jnp.pad(w_dkv, ((0, 0), (0, LANES - MLA_ROPE_DIM))).astype(BF16)
    return pl.pallas_call(
        functools.partial(_dkv_kernel, lora=lora),
        out_shape=[jax.ShapeDtypeStruct((m, lora), BF16), jax.ShapeDtypeStruct((m, LANES), BF16)],
        grid=(m // tm,),
        in_specs=[
            pl.BlockSpec((tm, d), lambda i: (i, 0)),
            pl.BlockSpec((d, lora + LANES), lambda i: (0, 0)),
            pl.BlockSpec((1, lora), lambda i: (0, 0)),
            pl.BlockSpec((tm, LANES), lambda i: (i % ns, 0)),
            pl.BlockSpec((tm, LANES), lambda i: (i % ns, 0)),
        ],
        out_specs=[pl.BlockSpec((tm, lora), lambda i: (i, 0)), pl.BlockSpec((tm, LANES), lambda i: (i, 0))],
        compiler_params=_params(1),
        name="mla_down_kv",
    )(xb, w, kv_norm.reshape(1, lora), cos, sin)


def _dq_kernel(h_ref, w_ref, g_ref, o_ref):
    acc = jnp.dot(h_ref[...], w_ref[...], preferred_element_type=F32)
    o_ref[...] = (acc * lax.rsqrt(jnp.mean(acc * acc, axis=-1, keepdims=True) + RMS_EPS) * g_ref[...]).astype(BF16)


def _mla_down_q(h, w_dq, q_norm):
    m, d = h.shape
    lora = w_dq.shape[1]
    tm = _tile(m, 1024)
    return pl.pallas_call(
        _dq_kernel,
        out_shape=jax.ShapeDtypeStruct((m, lora), BF16),
        grid=(m // tm,),
        in_specs=[
            pl.BlockSpec((tm, d), lambda i: (i, 0)),
            pl.BlockSpec((d, lora), lambda i: (0, 0)),
            pl.BlockSpec((1, lora), lambda i: (0, 0)),
        ],
        out_specs=pl.BlockSpec((tm, lora), lambda i: (i, 0)),
        compiler_params=_params(1),
        name="mla_down_q",
    )(h, w_dq.astype(BF16), q_norm.reshape(1, lora))


def _uq_kernel(a_ref, w_ref, cos_ref, sin_ref, o_ref, *, scale, heads_per_tile):
    acc = jnp.dot(a_ref[...], w_ref[...], preferred_element_type=F32) * scale
    cos, sin = cos_ref[...], sin_ref[...]
    for g in range(heads_per_tile):
        lo = g * MLA_QK_PAD
        o_ref[:, lo:lo + HEAD_DIM] = acc[:, lo:lo + HEAD_DIM].astype(BF16)
        o_ref[:, lo + HEAD_DIM:lo + MLA_QK_PAD] = _rope_slab(
            acc[:, lo + HEAD_DIM:lo + MLA_QK_PAD], cos, sin).astype(BF16)


def _mla_up_q(qc, w_uq, cos, sin, batch, heads, scale):
    m, lora = qc.shape
    s = m // batch
    per_head = HEAD_DIM + MLA_ROPE_DIM
    w = jnp.pad(w_uq.reshape(lora, heads, per_head), ((0, 0), (0, 0), (0, MLA_QK_PAD - per_head)))
    w = w.reshape(lora, heads * MLA_QK_PAD).astype(BF16)
    n = heads * MLA_QK_PAD
    tm = _tile(s, 1024)
    ns = s // tm
    tn = _tile(n, 1024)
    return pl.pallas_call(
        functools.partial(_uq_kernel, scale=scale, heads_per_tile=tn // MLA_QK_PAD),
        out_shape=jax.ShapeDtypeStruct((m, n), BF16),
        grid=(m // tm, n // tn),
        in_specs=[
            pl.BlockSpec((tm, lora), lambda i, j: (i, 0)),
            pl.BlockSpec((lora, tn), lambda i, j: (0, j)),
            pl.BlockSpec((tm, LANES), lambda i, j: (i % ns, 0)),
            pl.BlockSpec((tm, LANES), lambda i, j: (i % ns, 0)),
        ],
        out_specs=pl.BlockSpec((tm, tn), lambda i, j: (i, j)),
        compiler_params=_params(2),
        name="mla_up_q",
    )(qc, w, cos, sin)


def _ffn_up_kernel(h_ref, wa_ref, wg_ref, cwa_ref, cwg_ref, cba_ref, cbg_ref, wd_ref, o_ref, wdb_ref,
                   wab_ref, wgb_ref, halo_a_ref, halo_g_ref, *, tm, tiles_per_seq):
    i = pl.program_id(1)

    @pl.when(i == 0)
    def _():
        wab_ref[...] = wa_ref[...].astype(BF16)
        wgb_ref[...] = wg_ref[...].astype(BF16)
        wdb_ref[...] = wd_ref[...].astype(BF16)

    seq_start = (i % tiles_per_seq) == 0
    h = h_ref[...]
    row = lax.broadcasted_iota(jnp.int32, (tm, wab_ref.shape[1]), 0)

    def causal_conv(w_ref, cw_ref, cb_ref, halo_ref):
        u = jnp.dot(h, w_ref[...], preferred_element_type=F32)
        prev = jnp.where(seq_start, 0.0, halo_ref[...])
        halo_ref[...] = u[tm - SUBLANES:, :]
        p1 = prev[SUBLANES - 1:SUBLANES, :]
        p2 = prev[SUBLANES - 2:SUBLANES - 1, :]
        u1 = jnp.where(row == 0, p1, pltpu.roll(u, 1, axis=0))
        u2 = jnp.where(row == 0, p2, jnp.where(row == 1, p1, pltpu.roll(u, 2, axis=0)))
        cw = cw_ref[...]
        return cb_ref[...] + cw[0:1, :] * u2 + cw[1:2, :] * u1 + cw[2:3, :] * u

    a = causal_conv(wab_ref, cwa_ref, cba_ref, halo_a_ref)
    g = causal_conv(wgb_ref, cwg_ref, cbg_ref, halo_g_ref)
    o_ref[...] = (g * jax.nn.sigmoid(g) * a).astype(o_ref.dtype)


def _ffn_up(h, w_up, conv_w, conv_b, w_down, batch):
    m, d = h.shape
    ff = w_up.shape[1] // 2
    s = m // batch
    tm = _tile(s, 1024)
    tn = _tile(ff, 256)
    nf = ff // tn
    a_col = lambda j, i: (0, j)
    g_col = lambda j, i: (0, nf + j)
    down_rows = pl.BlockSpec((tn, w_down.shape[1]), lambda j, i: (j, 0))
    cb = conv_b.reshape(1, 2 * ff)
    return pl.pallas_call(
        functools.partial(_ffn_up_kernel, tm=tm, tiles_per_seq=s // tm),
        out_shape=[jax.ShapeDtypeStruct((m, ff), BF16), jax.ShapeDtypeStruct(w_down.shape, BF16)],
        grid=(nf, m // tm),
        in_specs=[
            pl.BlockSpec((tm, d), lambda j, i: (i, 0)),
            pl.BlockSpec((d, tn), a_col),
            pl.BlockSpec((d, tn), g_col),
            pl.BlockSpec((CONV_WIDTH, tn), a_col),
            pl.BlockSpec((CONV_WIDTH, tn), g_col),
            pl.BlockSpec((1, tn), a_col),
            pl.BlockSpec((1, tn), g_col),
            down_rows,
        ],
        out_specs=[pl.BlockSpec((tm, tn), lambda j, i: (i, j)), down_rows],
        scratch_shapes=[pltpu.VMEM((d, tn), BF16), pltpu.VMEM((d, tn), BF16),
                        pltpu.VMEM((SUBLANES, tn), F32), pltpu.VMEM((SUBLANES, tn), F32)],
        compiler_params=_params(2),
        name="ffn_up_conv_gate",
    )(h, w_up, w_up, conv_w, conv_w, cb, cb, w_down)


def _conv_ffn(h, w_up, conv_w, conv_b, w_down, batch):
    act, w_down_bf16 = _ffn_up(h, w_up, conv_w, conv_b, w_down, batch)
    return _mm(act, w_down_bf16, F32, 512, 512, name="ffn_down")


def kernel(x, c, ada_w, ada_b, ln_g, ln_b, fox_w_qkv, fox_w_f, fox_b_f, fox_w_o, mla_w_dq, mla_q_norm,
           mla_w_uq, mla_w_o, mla_w_dkv, mla_kv_norm, mla_w_ukv, ffn_w_up, ffn_conv_w, ffn_conv_b, ffn_w_down):
    batch, seq, d = x.shape
    depth = ada_w.shape[0]
    n_a_layers = fox_w_qkv.shape[0]
    heads = d // HEAD_DIM
    alpha = (2.0 * depth) ** 0.25
    fox_scale = LOG2E * HEAD_DIM ** -0.5
    mla_scale = LOG2E * (HEAD_DIM + MLA_ROPE_DIM) ** -0.5

    ada = _ada_all(c, ada_w, ada_b)
    mods = [_split_ada(ada[lm], d) for lm in range(depth * 2)]
    cos, sin = _rope_tables(seq)

    x2d = x.reshape(batch * seq, d)
    shift, scale, _ = mods[0]
    h = _modulate(x2d, scale, shift, batch)
    kv = k_rope = None
    for layer in range(depth):
        gate_mix = mods[2 * layer][2]
        shift_ffn, scale_ffn, gate_ffn = mods[2 * layer + 1]
        if layer < n_a_layers:
            a = layer
            qkv = _mm(h, fox_w_qkv[a].astype(BF16), BF16, 1024, 1024,
                      scale=fox_scale, n_scaled_cols=heads * HEAD_DIM, name="fox_qkv")
            qaug, kaug = _forget_bias_columns(h, fox_w_f[a], fox_b_f[a], batch)
            attn = _fox_attention(qkv, qaug, kaug, batch, heads)
            mix = _mm(attn, fox_w_o[a].astype(BF16), F32, 1024, 1024, name="fox_out")
        else:
            j = layer - n_a_layers
            qc = _mla_down_q(h, mla_w_dq[j], mla_q_norm[j])
            q_cat = _mla_up_q(qc, mla_w_uq[j], cos, sin, batch, heads, mla_scale)
            attn = _mla_attention(q_cat, kv, k_rope, batch, heads)
            mix = _mm(attn, mla_w_o[j].astype(BF16), F32, 1024, 1024, name="mla_out")
        x2d, h = _residual_ln(x2d, mix, gate_mix, ln_g[layer, 0], ln_b[layer, 0], batch, alpha,
                              next_mod=(scale_ffn, shift_ffn))
        ffn = _conv_ffn(h, ffn_w_up[layer], ffn_conv_w[layer], ffn_conv_b[layer], ffn_w_down[layer], batch)
        if layer + 1 < depth:
            shift_n, scale_n, _ = mods[2 * layer + 2]
            shares_kv = layer + 1 == n_a_layers
            outs = _residual_ln(x2d, ffn, gate_ffn, ln_g[layer, 1], ln_b[layer, 1], batch, alpha,
                                next_mod=(scale_n, shift_n), with_xb=shares_kv)
            x2d, h = outs[0], outs[1]
            if shares_kv:
                c_kv, k_rope = _mla_down_kv(outs[2], mla_w_dkv, mla_kv_norm, cos, sin, batch)
                kv = _mm(c_kv, mla_w_ukv.astype(BF16), BF16, 1024, 1024, name="mla_up_kv")
        else:
            (x2d,) = _residual_ln(x2d, ffn, gate_ffn, ln_g[layer, 1], ln_b[layer, 1], batch, alpha)
    return x2d.reshape(batch, seq, d)
```

```python
import functools
import math

import jax
import jax.numpy as jnp
from jax import lax
from jax.experimental import pallas as pl
from jax.experimental.pallas import tpu as pltpu

F32 = jnp.float32
BF16 = jnp.bfloat16

HEAD_DIM = 128
MLA_ROPE_DIM = 64
MLA_QK_PAD = 256
CHUNK = 64
CONV_WIDTH = 3
ROPE_THETA = 10000.0
LN_EPS = 1e-5
RMS_EPS = 1e-6
LOG2E = math.log2(math.e)
LANES = 128
SUBLANES = 8
VMEM_LIMIT_BYTES = 56 * 1024 * 1024
ATTN_TQ = 1024
ATTN_TK = 512
HEADS_PER_STEP = 2
N_BIAS_PIECES = 3


def _params(n_grid_dims):
    return pltpu.CompilerParams(
        dimension_semantics=("arbitrary",) * n_grid_dims,
        vmem_limit_bytes=VMEM_LIMIT_BYTES)


def _tile(n, want):
    t = min(n, want)
    while n % t:
        t //= 2
    return t


def _ada_kernel(c_ref, w_ref, b_ref, o_ref):
    c = c_ref[...]
    c_act = (c * jax.nn.sigmoid(c)).astype(BF16)
    o_ref[0] = jnp.dot(c_act, w_ref[0, 0].astype(BF16), preferred_element_type=F32) + b_ref[0]


def _ada_all(c, ada_w, ada_b):
    depth, two, d, d3 = ada_w.shape
    b = c.shape[0]
    c_pad = jnp.pad(c, ((0, SUBLANES - b), (0, 0)))
    tn = _tile(d3, 512)
    out = pl.pallas_call(
        _ada_kernel,
        out_shape=jax.ShapeDtypeStruct((depth * two, SUBLANES, d3), F32),
        grid=(depth * two, d3 // tn),
        in_specs=[
            pl.BlockSpec((SUBLANES, d), lambda lm, j: (0, 0)),
            pl.BlockSpec((1, 1, d, tn), lambda lm, j: (lm // 2, lm % 2, 0, j)),
            pl.BlockSpec((1, 1, tn), lambda lm, j: (lm, 0, j)),
        ],
        out_specs=pl.BlockSpec((1, SUBLANES, tn), lambda lm, j: (lm, 0, j)),
        compiler_params=_params(2),
        name="ada_ln",
    )(c_pad, ada_w, ada_b.reshape(depth * two, 1, d3))
    return out[:, :b]


def _split_ada(ada_lm, d):
    return (ada_lm[:, None, 0:d], ada_lm[:, None, d:2 * d], ada_lm[:, None, 2 * d:3 * d])


def _modulate_kernel(x_ref, scale_ref, shift_ref, h_ref):
    h_ref[...] = (x_ref[...] * (1.0 + scale_ref[0]) + shift_ref[0]).astype(h_ref.dtype)


def _modulate(x2d, scale, shift, batch):
    m, d = x2d.shape
    s = m // batch
    ts = _tile(s, 512)
    ns = s // ts
    return pl.pallas_call(
        _modulate_kernel,
        out_shape=jax.ShapeDtypeStruct((m, d), BF16),
        grid=(batch, ns),
        in_specs=[
            pl.BlockSpec((ts, d), lambda b, i: (b * ns + i, 0)),
            pl.BlockSpec((1, 1, d), lambda b, i: (b, 0, 0)),
            pl.BlockSpec((1, 1, d), lambda b, i: (b, 0, 0)),
        ],
        out_specs=pl.BlockSpec((ts, d), lambda b, i: (b * ns + i, 0)),
        compiler_params=_params(2),
        name="modulate",
    )(x2d, scale, shift)


def _ln_kernel(*refs, alpha, with_h):
    x_ref, mix_ref, gate_ref, g_ref, b_ref = refs[:5]
    refs = refs[5:]
    if with_h:
        scale_ref, shift_ref = refs[:2]
        refs = refs[2:]
    xo_ref = refs[0]
    refs = refs[1:]
    y = alpha * x_ref[...] + (1.0 + gate_ref[0]) * mix_ref[...]
    mu = jnp.mean(y, axis=-1, keepdims=True)
    dlt = y - mu
    var = jnp.mean(dlt * dlt, axis=-1, keepdims=True)
    xn = dlt * lax.rsqrt(var + LN_EPS) * g_ref[...] + b_ref[...]
    xo_ref[...] = xn
    if with_h:
        refs[0][...] = (xn * (1.0 + scale_ref[0]) + shift_ref[0]).astype(BF16)


def _residual_ln(x2d, mix, gate, ln_g, ln_b, batch, alpha, next_mod=None):
    m, d = x2d.shape
    s = m // batch
    ts = _tile(s, 256)
    ns = s // ts
    row = pl.BlockSpec((ts, d), lambda b, i: (b * ns + i, 0))
    per_batch = pl.BlockSpec((1, 1, d), lambda b, i: (b, 0, 0))
    vec = pl.BlockSpec((1, d), lambda b, i: (0, 0))
    with_h = next_mod is not None
    args = [x2d, mix, gate, ln_g.reshape(1, d), ln_b.reshape(1, d)]
    in_specs = [row, row, per_batch, vec, vec]
    out_shape = [jax.ShapeDtypeStruct((m, d), F32)]
    out_specs = [row]
    if with_h:
        args += [next_mod[0], next_mod[1]]
        in_specs += [per_batch, per_batch]
        out_shape.append(jax.ShapeDtypeStruct((m, d), BF16))
        out_specs.append(row)
    return pl.pallas_call(
        functools.partial(_ln_kernel, alpha=alpha, with_h=with_h),
        out_shape=out_shape,
        grid=(batch, ns),
        in_specs=in_specs,
        out_specs=out_specs,
        compiler_params=_params(2),
        name="residual_ln",
    )(*args)


def _mm_kernel(a_ref, w_ref, o_ref, *, scale, n_scaled_tiles):
    acc = jnp.dot(a_ref[...], w_ref[...], preferred_element_type=F32)
    if n_scaled_tiles:
        acc = acc * jnp.where(pl.program_id(1) < n_scaled_tiles, scale, 1.0)
    o_ref[...] = acc.astype(o_ref.dtype)


def _mm(a, w, out_dtype, tm, tn, scale=1.0, n_scaled_cols=0, name="mm"):
    m, k = a.shape
    n = w.shape[1]
    tm, tn = _tile(m, tm), _tile(n, tn)
    if n_scaled_cols:
        tn = _tile(n_scaled_cols, tn)
        assert n % tn == 0
    return pl.pallas_call(
        functools.partial(_mm_kernel, scale=scale, n_scaled_tiles=n_scaled_cols // tn),
        out_shape=jax.ShapeDtypeStruct((m, n), out_dtype),
        grid=(m // tm, n // tn),
        in_specs=[
            pl.BlockSpec((tm, k), lambda i, j: (i, 0)),
            pl.BlockSpec((k, tn), lambda i, j: (0, j)),
        ],
        out_specs=pl.BlockSpec((tm, tn), lambda i, j: (i, j)),
        compiler_params=_params(2),
        name=name,
    )(a, w)


def _split3(x):
    hi = x.astype(BF16)
    r1 = x - hi.astype(F32)
    mid = r1.astype(BF16)
    lo = (r1 - mid.astype(F32)).astype(BF16)
    return hi, mid, lo


def _fgate_kernel(h_ref, wf_ref, bf_ref, pq_ref, pk_ref, oneq_ref, onek_ref, qaug_ref, kaug_ref, carry_ref, *, tm):
    @pl.when(pl.program_id(1) == 0)
    def _():
        carry_ref[...] = jnp.zeros_like(carry_ref)

    f = jnp.dot(h_ref[...], wf_ref[...], preferred_element_type=F32) + bf_ref[...]
    log_f = jnp.minimum(f, 0.0) - jnp.log1p(jnp.exp(-jnp.abs(f)))
    row = lax.broadcasted_iota(jnp.int32, (tm, tm), 0)
    col = lax.broadcasted_iota(jnp.int32, (tm, tm), 1)
    tril = jnp.where(col <= row, 1.0, 0.0).astype(BF16)
    cum = sum(jnp.dot(tril, piece, preferred_element_type=F32) for piece in _split3(log_f)) + carry_ref[...]
    carry_ref[...] = cum[tm - 1:tm, :]
    pieces = jnp.concatenate(_split3(cum * LOG2E), axis=1)
    qaug_ref[...] = (jnp.dot(pieces, pq_ref[...], preferred_element_type=F32) + oneq_ref[...]).astype(BF16)
    kaug_ref[...] = (jnp.dot(pieces, pk_ref[...], preferred_element_type=F32) + onek_ref[...]).astype(BF16)


def _forget_bias_columns(h, w_f, b_f, batch):
    m, d = h.shape
    s = m // batch
    heads = w_f.shape[1]
    tm = _tile(s, 256)
    ns = s // tm
    n = heads * LANES
    wf = jnp.pad(w_f, ((0, 0), (0, LANES - heads))).astype(BF16)
    bf = jnp.pad(b_f, (0, LANES - heads)).reshape(1, LANES)
    src = jnp.arange(N_BIAS_PIECES * LANES)[:, None]
    dst = jnp.arange(n)[None, :]
    same_head = (dst // LANES) == (src % LANES)
    piece = src // LANES
    pq = (same_head & (dst % LANES == piece)).astype(BF16)
    pk = -((same_head & (dst % LANES == piece + N_BIAS_PIECES)).astype(BF16))
    lane = dst % LANES
    oneq = ((lane >= N_BIAS_PIECES) & (lane < 2 * N_BIAS_PIECES)).astype(F32)
    onek = (lane < N_BIAS_PIECES).astype(F32)
    const = lambda shape: pl.BlockSpec(shape, lambda b, i: (0, 0))
    out_spec = pl.BlockSpec((tm, n), lambda b, i: (b * ns + i, 0))
    return pl.pallas_call(
        functools.partial(_fgate_kernel, tm=tm),
        out_shape=[jax.ShapeDtypeStruct((m, n), BF16), jax.ShapeDtypeStruct((m, n), BF16)],
        grid=(batch, ns),
        in_specs=[
            pl.BlockSpec((tm, d), lambda b, i: (b * ns + i, 0)),
            const((d, LANES)), const((1, LANES)),
            const((N_BIAS_PIECES * LANES, n)), const((N_BIAS_PIECES * LANES, n)),
            const((1, n)), const((1, n)),
        ],
        out_specs=[out_spec, out_spec],
        scratch_shapes=[pltpu.VMEM((1, LANES), F32)],
        compiler_params=_params(2),
        name="forget_bias_columns",
    )(h, wf, bf, pq, pk, oneq, onek)


def _attn_kernel(*refs, n_q_parts, tq, tk, chunk):
    n_in = n_q_parts + 3
    heads = []
    for hd in range(HEADS_PER_STEP):
        ins = refs[hd * n_in:(hd + 1) * n_in]
        scr = refs[HEADS_PER_STEP * n_in + 1 + hd * 5:HEADS_PER_STEP * n_in + 1 + (hd + 1) * 5]
        heads.append(dict(q=ins[:n_q_parts], k=ins[n_q_parts], ka=ins[n_q_parts + 1], v=ins[n_q_parts + 2],
                          s=scr[0:2], m=scr[2], l=scr[3], acc=scr[4]))
    o_ref = refs[HEADS_PER_STEP * n_in]
    qi = pl.program_id(2)

    def qk(hd, j, slot):
        q = hd["q"][0][...] if n_q_parts == 1 else jnp.concatenate([r[...] for r in hd["q"]], axis=1)
        start = pl.multiple_of(j * tk, tk)
        kk = jnp.concatenate([hd["k"][pl.ds(start, tk), :], hd["ka"][pl.ds(start, tk), :]], axis=1)
        hd["s"][slot][...] = lax.dot_general(kk, q, (((1,), (1,)), ((), ())), preferred_element_type=F32)

    def update(hd, s, j):
        start = pl.multiple_of(j * tk, tk)
        v = hd["v"][pl.ds(start, tk), :]
        m = hd["m"][...]
        m_new = jnp.maximum(m, jnp.max(s, axis=0, keepdims=True))
        alpha = jnp.exp2(m - m_new)
        p = jnp.exp2(s - m_new)
        hd["l"][...] = alpha * hd["l"][...] + jnp.sum(p, axis=0, keepdims=True)
        pv = lax.dot_general(v, p.astype(BF16), (((0,), (0,)), ((), ())), preferred_element_type=F32)
        hd["acc"][...] = alpha * hd["acc"][...] + pv
        hd["m"][...] = m_new

    for hd in heads:
        hd["m"][...] = jnp.full_like(hd["m"], -jnp.inf)
        hd["l"][...] = jnp.zeros_like(hd["l"])
        hd["acc"][...] = jnp.zeros_like(hd["acc"])
        qk(hd, 0, 0)

    def body(i, _):
        for slot in range(2):
            for hd in heads:
                qk(hd, 2 * i + 1 + slot, 1 - slot)
                update(hd, hd["s"][slot][...], 2 * i + slot)
        return 0

    lax.fori_loop(0, qi, body, 0)
    key = lax.broadcasted_iota(jnp.int32, (tk, tq), 0) // chunk
    qry = lax.broadcasted_iota(jnp.int32, (tk, tq), 1) // chunk
    for hd in heads:
        qk(hd, 2 * qi + 1, 1)
        update(hd, jnp.where(key <= qry, hd["s"][0][...], -jnp.inf), 2 * qi)
    for n, hd in enumerate(heads):
        update(hd, jnp.where(key + tk // chunk <= qry, hd["s"][1][...], -jnp.inf), 2 * qi + 1)
        o_ref[:, n * HEAD_DIM:(n + 1) * HEAD_DIM] = (hd["acc"][...] / hd["l"][...]).T.astype(o_ref.dtype)


def _attention(head_args, head_specs, n_q_parts, m, s, heads, chunk, name):
    tq, tk = ATTN_TQ, ATTN_TK
    assert s % tq == 0 and tq == 2 * tk and tk % chunk == 0 and heads % HEADS_PER_STEP == 0
    nq = s // tq
    batch = m // s
    per_head_scratch = [pltpu.VMEM((tk, tq), F32), pltpu.VMEM((tk, tq), F32),
                        pltpu.VMEM((1, tq), F32), pltpu.VMEM((1, tq), F32), pltpu.VMEM((HEAD_DIM, tq), F32)]
    return pl.pallas_call(
        functools.partial(_attn_kernel, n_q_parts=n_q_parts, tq=tq, tk=tk, chunk=chunk),
        out_shape=jax.ShapeDtypeStruct((m, heads * HEAD_DIM), BF16),
        grid=(batch, heads // HEADS_PER_STEP, nq),
        in_specs=[spec for par in range(HEADS_PER_STEP) for spec in head_specs(par)],
        out_specs=pl.BlockSpec((tq, HEADS_PER_STEP * HEAD_DIM), lambda b, g, i: (b * nq + i, g)),
        scratch_shapes=per_head_scratch * HEADS_PER_STEP,
        compiler_params=_params(3),
        name=name,
    )(*(list(head_args) * HEADS_PER_STEP))


def _fox_attention(qkv, qaug, kaug, batch, heads):
    m = qkv.shape[0]
    s = m // batch
    nq = s // ATTN_TQ

    def specs(par):
        head = lambda g: HEADS_PER_STEP * g + par
        q_tile = lambda shift: pl.BlockSpec((ATTN_TQ, HEAD_DIM), lambda b, g, i: (b * nq + i, shift + head(g)))
        seq = lambda shift: pl.BlockSpec((s, HEAD_DIM), lambda b, g, i: (b, shift + head(g)))
        return [q_tile(0), q_tile(0), seq(heads), seq(0), seq(2 * heads)]

    return _attention((qkv, qaug, qkv, kaug, qkv), specs, 2, m, s, heads, 1, "fox_attention")


def _mla_attention(q_cat, kv, k_rope, batch, heads):
    m = q_cat.shape[0]
    s = m // batch
    nq = s // ATTN_TQ

    def specs(par):
        head = lambda g: HEADS_PER_STEP * g + par
        return [pl.BlockSpec((ATTN_TQ, MLA_QK_PAD), lambda b, g, i: (b * nq + i, head(g))),
                pl.BlockSpec((s, HEAD_DIM), lambda b, g, i: (b, 2 * head(g))),
                pl.BlockSpec((s, LANES), lambda b, g, i: (b, 0)),
                pl.BlockSpec((s, HEAD_DIM), lambda b, g, i: (b, 2 * head(g) + 1))]

    return _attention((q_cat, kv, k_rope, kv), specs, 1, m, s, heads, CHUNK, "mla_attention")


def _rope_tables(s):
    half = MLA_ROPE_DIM // 2
    inv = ROPE_THETA ** (-jnp.arange(0, MLA_ROPE_DIM, 2, dtype=F32) / MLA_ROPE_DIM)
    ang = jnp.arange(s, dtype=F32)[:, None] * inv[None, :]
    cos, sin = jnp.cos(ang), jnp.sin(ang)
    zeros = jnp.zeros((s, LANES - 2 * half), F32)
    return (jnp.concatenate([cos, cos, zeros], axis=1), jnp.concatenate([-sin, sin, zeros], axis=1))


def _rope_slab(x, cos, sin):
    half = MLA_ROPE_DIM // 2
    lane = lax.broadcasted_iota(jnp.int32, x.shape, 1)
    swapped = jnp.where(lane < half, pltpu.roll(x, LANES - half, axis=1), pltpu.roll(x, half, axis=1))
    return x * cos + swapped * sin


def _dkv_kernel(x_ref, w_ref, g_ref, cos_ref, sin_ref, ckv_ref, kr_ref, *, lora):
    acc = jnp.dot(x_ref[...].astype(BF16), w_ref[...], preferred_element_type=F32)
    c = acc[:, :lora]
    ckv_ref[...] = (c * lax.rsqrt(jnp.mean(c * c, axis=-1, keepdims=True) + RMS_EPS) * g_ref[...]).astype(BF16)
    kr_ref[...] = _rope_slab(acc[:, lora:], cos_ref[...], sin_ref[...]).astype(BF16)


def _mla_down_kv(x2d, w_dkv, kv_norm, cos, sin, batch):
    m, d = x2d.shape
    lora = kv_norm.shape[0]
    s = m // batch
    tm = _tile(s, 512)
    ns = s // tm
    w = jnp.pad(w_dkv, ((0, 0), (0, LANES - MLA_ROPE_DIM))).astype(BF16)
    return pl.pallas_call(
        functools.partial(_dkv_kernel, lora=lora),
        out_shape=[jax.ShapeDtypeStruct((m, lora), BF16), jax.ShapeDtypeStruct((m, LANES), BF16)],
        grid=(m // tm,),
        in_specs=[
            pl.BlockSpec((tm, d), lambda i: (i, 0)),
            pl.BlockSpec((d, lora + LANES), lambda i: (0, 0)),
            pl.BlockSpec((1, lora), lambda i: (0, 0)),
            pl.BlockSpec((tm, LANES), lambda i: (i % ns, 0)),
            pl.BlockSpec((tm, LANES), lambda i: (i % ns, 0)),
        ],
        out_specs=[pl.BlockSpec((tm, lora), lambda i: (i, 0)), pl.BlockSpec((tm, LANES), lambda i: (i, 0))],
        compiler_params=_params(1),
        name="mla_down_kv",
    )(x2d, w, kv_norm.reshape(1, lora), cos, sin)


def _dq_kernel(h_ref, w_ref, g_ref, o_ref):
    acc = jnp.dot(h_ref[...], w_ref[...], preferred_element_type=F32)
    o_ref[...] = (acc * lax.rsqrt(jnp.mean(acc * acc, axis=-1, keepdims=True) + RMS_EPS) * g_ref[...]).astype(BF16)


def _mla_down_q(h, w_dq, q_norm):
    m, d = h.shape
    lora = w_dq.shape[1]
    tm = _tile(m, 1024)
    return pl.pallas_call(
        _dq_kernel,
        out_shape=jax.ShapeDtypeStruct((m, lora), BF16),
        grid=(m // tm,),
        in_specs=[
            pl.BlockSpec((tm, d), lambda i: (i, 0)),
            pl.BlockSpec((d, lora), lambda i: (0, 0)),
            pl.BlockSpec((1, lora), lambda i: (0, 0)),
        ],
        out_specs=pl.BlockSpec((tm, lora), lambda i: (i, 0)),
        compiler_params=_params(1),
        name="mla_down_q",
    )(h, w_dq.astype(BF16), q_norm.reshape(1, lora))


def _uq_kernel(a_ref, w_ref, cos_ref, sin_ref, o_ref, *, scale, heads_per_tile):
    acc = jnp.dot(a_ref[...], w_ref[...], preferred_element_type=F32) * scale
    cos, sin = cos_ref[...], sin_ref[...]
    for g in range(heads_per_tile):
        lo = g * MLA_QK_PAD
        o_ref[:, lo:lo + HEAD_DIM] = acc[:, lo:lo + HEAD_DIM].astype(BF16)
        o_ref[:, lo + HEAD_DIM:lo + MLA_QK_PAD] = _rope_slab(
            acc[:, lo + HEAD_DIM:lo + MLA_QK_PAD], cos, sin).astype(BF16)


def _mla_up_q(qc, w_uq, cos, sin, batch, heads, scale):
    m, lora = qc.shape
    s = m // batch
    per_head = HEAD_DIM + MLA_ROPE_DIM
    w = jnp.pad(w_uq.reshape(lora, heads, per_head), ((0, 0), (0, 0), (0, MLA_QK_PAD - per_head)))
    w = w.reshape(lora, heads * MLA_QK_PAD).astype(BF16)
    n = heads * MLA_QK_PAD
    tm = _tile(s, 1024)
    ns = s // tm
    tn = _tile(n, 1024)
    return pl.pallas_call(
        functools.partial(_uq_kernel, scale=scale, heads_per_tile=tn // MLA_QK_PAD),
        out_shape=jax.ShapeDtypeStruct((m, n), BF16),
        grid=(m // tm, n // tn),
        in_specs=[
            pl.BlockSpec((tm, lora), lambda i, j: (i, 0)),
            pl.BlockSpec((lora, tn), lambda i, j: (0, j)),
            pl.BlockSpec((tm, LANES), lambda i, j: (i % ns, 0)),
            pl.BlockSpec((tm, LANES), lambda i, j: (i % ns, 0)),
        ],
        out_specs=pl.BlockSpec((tm, tn), lambda i, j: (i, j)),
        compiler_params=_params(2),
        name="mla_up_q",
    )(qc, w, cos, sin)


def _ffn_up_kernel(h_ref, wa_ref, wg_ref, cwa_ref, cwg_ref, cba_ref, cbg_ref, wd_ref, o_ref, wdb_ref,
                   wab_ref, wgb_ref, halo_a_ref, halo_g_ref, *, tm, tiles_per_seq):
    i = pl.program_id(1)

    @pl.when(i == 0)
    def _():
        wab_ref[...] = wa_ref[...].astype(BF16)
        wgb_ref[...] = wg_ref[...].astype(BF16)

    wdb_ref[...] = wd_ref[...].astype(BF16)

    seq_start = (i % tiles_per_seq) == 0
    h = h_ref[...]
    row = lax.broadcasted_iota(jnp.int32, (tm, wab_ref.shape[1]), 0)

    def causal_conv(w_ref, cw_ref, cb_ref, halo_ref):
        u = jnp.dot(h, w_ref[...], preferred_element_type=F32)
        prev = jnp.where(seq_start, 0.0, halo_ref[...])
        halo_ref[...] = u[tm - SUBLANES:, :]
        p1 = prev[SUBLANES - 1:SUBLANES, :]
        p2 = prev[SUBLANES - 2:SUBLANES - 1, :]
        u1 = jnp.where(row == 0, p1, pltpu.roll(u, 1, axis=0))
        u2 = jnp.where(row == 0, p2, jnp.where(row == 1, p1, pltpu.roll(u, 2, axis=0)))
        cw = cw_ref[...]
        return cb_ref[...] + cw[0:1, :] * u2 + cw[1:2, :] * u1 + cw[2:3, :] * u

    a = causal_conv(wab_ref, cwa_ref, cba_ref, halo_a_ref)
    g = causal_conv(wgb_ref, cwg_ref, cbg_ref, halo_g_ref)
    o_ref[...] = (g * jax.nn.sigmoid(g) * a).astype(o_ref.dtype)


def _ffn_up(h, w_up, conv_w, conv_b, w_down, layer, batch):
    m, d = h.shape
    depth, ff, dm = w_down.shape
    s = m // batch
    tm = _tile(s, 1024)
    tn = _tile(ff, 256)
    nf = ff // tn
    n_i = m // tm
    down_rows = ff // (nf * n_i)
    assert down_rows * nf * n_i == ff and down_rows % (2 * SUBLANES) == 0
    a_col = lambda j, i: (layer, 0, j)
    g_col = lambda j, i: (layer, 0, nf + j)
    cb = conv_b.reshape(depth, 1, 2 * ff)
    return pl.pallas_call(
        functools.partial(_ffn_up_kernel, tm=tm, tiles_per_seq=s // tm),
        out_shape=[jax.ShapeDtypeStruct((m, ff), BF16), jax.ShapeDtypeStruct((ff, dm), BF16)],
        grid=(nf, n_i),
        in_specs=[
            pl.BlockSpec((tm, d), lambda j, i: (i, 0)),
            pl.BlockSpec((None, d, tn), a_col),
            pl.BlockSpec((None, d, tn), g_col),
            pl.BlockSpec((None, CONV_WIDTH, tn), a_col),
            pl.BlockSpec((None, CONV_WIDTH, tn), g_col),
            pl.BlockSpec((None, 1, tn), a_col),
            pl.BlockSpec((None, 1, tn), g_col),
            pl.BlockSpec((None, down_rows, dm), lambda j, i: (layer, j * n_i + i, 0)),
        ],
        out_specs=[pl.BlockSpec((tm, tn), lambda j, i: (i, j)),
                   pl.BlockSpec((down_rows, dm), lambda j, i: (j * n_i + i, 0))],
        scratch_shapes=[pltpu.VMEM((d, tn), BF16), pltpu.VMEM((d, tn), BF16),
                        pltpu.VMEM((SUBLANES, tn), F32), pltpu.VMEM((SUBLANES, tn), F32)],
        compiler_params=_params(2),
        name="ffn_up_conv_gate",
    )(h, w_up, w_up, conv_w, conv_w, cb, cb, w_down)


def _conv_ffn(h, w_up, conv_w, conv_b, w_down, layer, batch):
    act, w_down_bf16 = _ffn_up(h, w_up, conv_w, conv_b, w_down, layer, batch)
    return _mm(act, w_down_bf16, F32, 512, 512, name="ffn_down")


def kernel(x, c, ada_w, ada_b, ln_g, ln_b, fox_w_qkv, fox_w_f, fox_b_f, fox_w_o, mla_w_dq, mla_q_norm,
           mla_w_uq, mla_w_o, mla_w_dkv, mla_kv_norm, mla_w_ukv, ffn_w_up, ffn_conv_w, ffn_conv_b, ffn_w_down):
    batch, seq, d = x.shape
    depth = ada_w.shape[0]
    n_a_layers = fox_w_qkv.shape[0]
    heads = d // HEAD_DIM
    alpha = (2.0 * depth) ** 0.25
    fox_scale = LOG2E * HEAD_DIM ** -0.5
    mla_scale = LOG2E * (HEAD_DIM + MLA_ROPE_DIM) ** -0.5

    ada = _ada_all(c, ada_w, ada_b)
    mods = [_split_ada(ada[lm], d) for lm in range(depth * 2)]
    cos, sin = _rope_tables(seq)

    x2d = x.reshape(batch * seq, d)
    shift, scale, _ = mods[0]
    h = _modulate(x2d, scale, shift, batch)
    kv = k_rope = None
    for layer in range(depth):
        gate_mix = mods[2 * layer][2]
        shift_ffn, scale_ffn, gate_ffn = mods[2 * layer + 1]
        if layer < n_a_layers:
            a = layer
            qkv = _mm(h, fox_w_qkv[a].astype(BF16), BF16, 1024, 1024,
                      scale=fox_scale, n_scaled_cols=heads * HEAD_DIM, name="fox_qkv")
            qaug, kaug = _forget_bias_columns(h, fox_w_f[a], fox_b_f[a], batch)
            attn = _fox_attention(qkv, qaug, kaug, batch, heads)
            mix = _mm(attn, fox_w_o[a].astype(BF16), F32, 1024, 1024, name="fox_out")
        else:
            j = layer - n_a_layers
            qc = _mla_down_q(h, mla_w_dq[j], mla_q_norm[j])
            q_cat = _mla_up_q(qc, mla_w_uq[j], cos, sin, batch, heads, mla_scale)
            attn = _mla_attention(q_cat, kv, k_rope, batch, heads)
            mix = _mm(attn, mla_w_o[j].astype(BF16), F32, 1024, 1024, name="mla_out")
        x2d, h = _residual_ln(x2d, mix, gate_mix, ln_g[layer, 0], ln_b[layer, 0], batch, alpha,
                              next_mod=(scale_ffn, shift_ffn))
        ffn = _conv_ffn(h, ffn_w_up, ffn_conv_w, ffn_conv_b, ffn_w_down, layer, batch)
        if layer + 1 < depth:
            shift_n, scale_n, _ = mods[2 * layer + 2]
            x2d, h = _residual_ln(x2d, ffn, gate_ffn, ln_g[layer, 1], ln_b[layer, 1], batch, alpha,
                                  next_mod=(scale_n, shift_n))
            if layer + 1 == n_a_layers:
                c_kv, k_rope = _mla_down_kv(x2d, mla_w_dkv, mla_kv_norm, cos, sin, batch)
                kv = _mm(c_kv, mla_w_ukv.astype(BF16), BF16, 1024, 1024, name="mla_up_kv")
        else:
            (x2d,) = _residual_ln(x2d, ffn, gate_ffn, ln_g[layer, 1], ln_b[layer, 1], batch, alpha)
    return x2d.reshape(batch, seq, d)
```
